```python
import jax
import jax.numpy as jnp
from jax import lax
import numpy as np

D_MODEL = 4096
BATCH = 4
SEQ = 2048
DEPTH = 1
DEC_BATCH = 128
DEC_SEQ = 4
PAST_LEN = 16384
PAGE_SIZE = 128

N_META = 16
Q_BLOCK = 128
HEAD_DIM = 128
FOX_HEADS = 16
FOX_KV_HEADS = 2
FOX_GROUP = FOX_HEADS // FOX_KV_HEADS
MLA_HEADS = 16
Q_LORA = 1024
KV_LORA = 512
NOPE_DIM = 128
ROPE_DIM = 64
V_DIM = 128
ROPE_BASE = 10000.0
D_FF = 4 * D_MODEL
EPS = 1e-6
FOX_W = FOX_HEADS * HEAD_DIM
MLA_W = MLA_HEADS * V_DIM
MIX_W = FOX_W + MLA_W
FOX_SCALE = HEAD_DIM ** -0.5
MLA_SCALE = (NOPE_DIM + ROPE_DIM) ** -0.5
IN_SIZES = (FOX_W, FOX_KV_HEADS * HEAD_DIM, FOX_KV_HEADS * HEAD_DIM, FOX_HEADS, Q_LORA, KV_LORA, ROPE_DIM)
IN_W = sum(IN_SIZES)

kernel_name = "hymba_fox_mla_decoder_step"


def rmsnorm(x, g):
    xf = x.astype(jnp.float32)
    y = xf * lax.rsqrt(jnp.mean(xf * xf, axis=-1, keepdims=True) + EPS)
    return (y * g.astype(jnp.float32)).astype(x.dtype)


def rope(x, pos):
    half = ROPE_DIM // 2
    inv = ROPE_BASE ** (-jnp.arange(half, dtype=jnp.float32) / half)
    ang = pos.astype(jnp.float32)[:, None] * inv[None, :]
    ang = ang.reshape((ang.shape[0],) + (1,) * (x.ndim - 3) + (half,))
    cos, sin = jnp.cos(ang), jnp.sin(ang)
    xf = x.astype(jnp.float32)
    x1, x2 = xf[..., :half], xf[..., half:]
    return jnp.concatenate([x1 * cos - x2 * sin, x1 * sin + x2 * cos], axis=-1).astype(x.dtype)


def mixer_inputs(h, pos, lw):
    B, T, _ = h.shape
    z = h @ lw['w_in']
    offs = np.cumsum(IN_SIZES)[:-1].tolist()
    q_f, k_f, v_f, f_logit, q_lat, c_kv, k_r = jnp.split(z, offs, axis=-1)
    q_f = q_f.reshape(B, T, FOX_KV_HEADS, FOX_GROUP, HEAD_DIM)
    k_f = k_f.reshape(B, T, FOX_KV_HEADS, HEAD_DIM)
    v_f = v_f.reshape(B, T, FOX_KV_HEADS, HEAD_DIM)
    logf = jax.nn.log_sigmoid((f_logit + lw['b_f']).astype(jnp.float32)).astype(h.dtype)
    q = (rmsnorm(q_lat, lw['g_q_lat']) @ lw['w_q_up']).reshape(B, T, MLA_HEADS, NOPE_DIM + ROPE_DIM)
    q_abs = jnp.einsum('bthn,hcn->bthc', q[..., :NOPE_DIM], lw['w_uk'])
    q_m = jnp.concatenate([q_abs, rope(q[..., NOPE_DIM:], pos)], axis=-1)
    c_kv = rmsnorm(c_kv, lw['g_kv_lat'])
    k_r = rope(k_r, pos)
    return q_f, k_f, v_f, logf, q_m, c_kv, k_r


def attend(q_f, cq, q_m, q_pos, k_f, v_f, ck, kcat, k_pos, w_uv):
    B, Tq = q_f.shape[:2]
    Tk = k_f.shape[1]
    mask = k_pos[None, :] <= q_pos[:, None]
    s = jnp.einsum('bqkgd,bskd->bkgqs', q_f, k_f).astype(jnp.float32) * FOX_SCALE
    bias = (cq.reshape(B, Tq, FOX_KV_HEADS, FOX_GROUP).transpose(0, 2, 3, 1)[..., None]
            - ck.reshape(B, Tk, FOX_KV_HEADS, FOX_GROUP).transpose(0, 2, 3, 1)[..., None, :])
    p = jax.nn.softmax(jnp.where(mask, s + bias, -jnp.inf), axis=-1).astype(v_f.dtype)
    o_f = jnp.einsum('bkgqs,bskd->bqkgd', p, v_f).reshape(B, Tq, FOX_W)
    s = jnp.einsum('bqhc,bsc->bhqs', q_m, kcat).astype(jnp.float32) * MLA_SCALE
    p = jax.nn.softmax(jnp.where(mask, s, -jnp.inf), axis=-1).astype(kcat.dtype)
    o_lat = jnp.einsum('bhqs,bsc->bqhc', p, kcat[..., :KV_LORA])
    o_m = jnp.einsum('bqhc,hcv->bqhv', o_lat, w_uv).reshape(B, Tq, MLA_W)
    return o_f, o_m


def finish_layer(x, o_f, o_m, lw):
    o = jnp.concatenate([rmsnorm(o_f, lw['g_fox_out']), rmsnorm(o_m, lw['g_mla_out'])], axis=-1)
    x = x + rmsnorm(o @ lw['w_o'], lw['g_post_mix'])
    u = jnp.square(jax.nn.relu(rmsnorm(x, lw['g_pre_mlp']) @ lw['w_up']))
    return x + rmsnorm(u @ lw['w_down'], lw['g_post_mlp'])


def prompt_layer(x, lw):
    B, T, _ = x.shape
    pos = jnp.arange(T)
    q_f, k_f, v_f, logf, q_m, c_kv, k_r = mixer_inputs(rmsnorm(x, lw['g_pre_mix']), pos, lw)
    C = jnp.cumsum(logf.astype(jnp.float32), axis=1)
    kcat = jnp.concatenate([c_kv, k_r], axis=-1)
    M = N_META
    mo_f, mo_m = attend(q_f[:, :M], C[:, :M], q_m[:, :M], pos[:M],
                        k_f[:, :M], v_f[:, :M], C[:, :M], kcat[:, :M], pos[:M], lw['w_uv'])
    nb = (T - M) // Q_BLOCK

    def blocks(a):
        return jnp.swapaxes(a[:, M:].reshape((B, nb, Q_BLOCK) + a.shape[2:]), 0, 1)

    def run(args):
        qf_b, cq_b, qm_b, qpos_b = args
        return attend(qf_b, cq_b, qm_b, qpos_b, k_f, v_f, C, kcat, pos, lw['w_uv'])

    bo_f, bo_m = lax.map(run, (blocks(q_f), blocks(C), blocks(q_m), pos[M:].reshape(nb, Q_BLOCK)))

    def unblocks(o):
        return jnp.swapaxes(o, 0, 1).reshape(B, T - M, o.shape[-1])

    o_f = jnp.concatenate([mo_f, unblocks(bo_f)], axis=1)
    o_m = jnp.concatenate([mo_m, unblocks(bo_m)], axis=1)
    return finish_layer(x, o_f, o_m, lw), (k_f, v_f, logf, c_kv, k_r)


def sample_layer(x, lw, c_k, c_v, c_lf, c_ckv, c_kr, page_table):
    B, S, _ = x.shape
    n_past = page_table.shape[1] * PAGE_SIZE
    pos = n_past + jnp.arange(S)
    q_f, k_f, v_f, logf, q_m, c_kv, k_r = mixer_inputs(rmsnorm(x, lw['g_pre_mix']), pos, lw)
    k_pos = jnp.arange(n_past + S)

    def one_seq(args):
        pt, qf, kf, vf, lf, qm, ckv, kr = args

        def past(cache):
            return cache[pt].reshape((n_past,) + cache.shape[2:])

        kf_all = jnp.concatenate([past(c_k), kf], axis=0)
        vf_all = jnp.concatenate([past(c_v), vf], axis=0)
        C = jnp.cumsum(jnp.concatenate([past(c_lf), lf], axis=0).astype(jnp.float32), axis=0)
        kcat = jnp.concatenate([jnp.concatenate([past(c_ckv), ckv], axis=0),
                                jnp.concatenate([past(c_kr), kr], axis=0)], axis=-1)
        o_f, o_m = attend(qf[None], C[None, n_past:], qm[None], pos, kf_all[None], vf_all[None],
                          C[None], kcat[None], k_pos, lw['w_uv'])
        return o_f[0], o_m[0]

    o_f, o_m = lax.map(one_seq, (page_table, q_f, k_f, v_f, logf, q_m, c_kv, k_r))
    return finish_layer(x, o_f, o_m, lw), (k_f, v_f, logf, c_kv, k_r)


def setup_inputs(seed: int = 0) -> dict:
    key = jax.random.key(seed)
    ks = jax.random.split(key, 26)
    n_pages = PAST_LEN // PAGE_SIZE
    n_used = DEC_BATCH * n_pages
    n_pool = n_used + max(1, n_used // 4)

    def nrm(k, shape, scale=1.0):
        return scale * jax.random.normal(k, shape, jnp.float32)

    def gain(k, n):
        return 1.0 + 0.02 * jax.random.normal(k, (DEPTH, n), jnp.float32)

    page_table = jax.random.permutation(ks[0], n_pool)[:n_used].reshape(DEC_BATCH, n_pages).astype(jnp.int32)
    b_f = jnp.linspace(2.0, 6.0, FOX_HEADS, dtype=jnp.float32)[None, :] + nrm(ks[1], (DEPTH, FOX_HEADS), 0.1)
    return {
        'x_prompt': nrm(ks[2], (BATCH, SEQ, D_MODEL)),
        'x_sample': nrm(ks[3], (DEC_BATCH, DEC_SEQ, D_MODEL)),
        'cache_fox_k': nrm(ks[4], (DEPTH, n_pool, PAGE_SIZE, FOX_KV_HEADS, HEAD_DIM)),
        'cache_fox_v': nrm(ks[5], (DEPTH, n_pool, PAGE_SIZE, FOX_KV_HEADS, HEAD_DIM)),
        'cache_fox_logf': jax.nn.log_sigmoid(nrm(ks[6], (DEPTH, n_pool, PAGE_SIZE, FOX_HEADS)) + 4.0),
        'cache_mla_ckv': nrm(ks[7], (DEPTH, n_pool, PAGE_SIZE, KV_LORA)),
        'cache_mla_krope': nrm(ks[8], (DEPTH, n_pool, PAGE_SIZE, ROPE_DIM)),
        'page_table': page_table,
        'meta_tokens': nrm(ks[9], (N_META, D_MODEL)),
        'g_pre_mix': gain(ks[10], D_MODEL),
        'w_in': nrm(ks[11], (DEPTH, D_MODEL, IN_W), D_MODEL ** -0.5),
        'b_f': b_f,
        'g_q_lat': gain(ks[12], Q_LORA),
        'w_q_up': nrm(ks[13], (DEPTH, Q_LORA, MLA_HEADS * (NOPE_DIM + ROPE_DIM)), Q_LORA ** -0.5),
        'w_uk': nrm(ks[14], (DEPTH, MLA_HEADS, KV_LORA, NOPE_DIM), KV_LORA ** -0.5),
        'g_kv_lat': gain(ks[15], KV_LORA),
        'w_uv': nrm(ks[16], (DEPTH, MLA_HEADS, KV_LORA, V_DIM), KV_LORA ** -0.5),
        'g_fox_out': gain(ks[17], FOX_W),
        'g_mla_out': gain(ks[18], MLA_W),
        'w_o': nrm(ks[19], (DEPTH, MIX_W, D_MODEL), MIX_W ** -0.5),
        'g_post_mix': gain(ks[20], D_MODEL),
        'g_pre_mlp': gain(ks[21], D_MODEL),
        'w_up': nrm(ks[22], (DEPTH, D_MODEL, D_FF), D_MODEL ** -0.5),
        'w_down': nrm(ks[23], (DEPTH, D_FF, D_MODEL), D_FF ** -0.5),
        'g_post_mlp': gain(ks[24], D_MODEL),
    }


def reference(x_prompt, x_sample, cache_fox_k, cache_fox_v, cache_fox_logf, cache_mla_ckv,
              cache_mla_krope, page_table, meta_tokens, g_pre_mix, w_in, b_f, g_q_lat, w_q_up,
              w_uk, g_kv_lat, w_uv, g_fox_out, g_mla_out, w_o, g_post_mix, g_pre_mlp, w_up,
              w_down, g_post_mlp):
    B = x_prompt.shape[0]
    meta = jnp.broadcast_to(meta_tokens[None].astype(x_prompt.dtype), (B, N_META, D_MODEL))
    xp = jnp.concatenate([meta, x_prompt], axis=1)
    xs = x_sample
    new_p = []
    new_s = []
    for l in range(DEPTH):
        lw = {'g_pre_mix': g_pre_mix[l], 'w_in': w_in[l], 'b_f': b_f[l], 'g_q_lat': g_q_lat[l],
              'w_q_up': w_q_up[l], 'w_uk': w_uk[l], 'g_kv_lat': g_kv_lat[l], 'w_uv': w_uv[l],
              'g_fox_out': g_fox_out[l], 'g_mla_out': g_mla_out[l], 'w_o': w_o[l],
              'g_post_mix': g_post_mix[l], 'g_pre_mlp': g_pre_mlp[l], 'w_up': w_up[l],
              'w_down': w_down[l], 'g_post_mlp': g_post_mlp[l]}
        xp, st_p = prompt_layer(xp, lw)
        xs, st_s = sample_layer(xs, lw, cache_fox_k[l], cache_fox_v[l], cache_fox_logf[l],
                                cache_mla_ckv[l], cache_mla_krope[l], page_table)
        new_p.append(st_p)
        new_s.append(st_s)

    def stack(states, i):
        return jnp.stack([st[i] for st in states], axis=0)

    y_prompt = xp[:, N_META:]
    return (y_prompt, xs,
            stack(new_p, 0), stack(new_p, 1), stack(new_p, 2), stack(new_p, 3), stack(new_p, 4),
            stack(new_s, 0), stack(new_s, 1), stack(new_s, 2), stack(new_s, 3), stack(new_s, 4))
```

```python
import functools

import jax
import jax.numpy as jnp
from jax import lax
from jax.experimental import pallas as pl
from jax.experimental.pallas import tpu as pltpu

D_MODEL = 4096
N_META = 16
HEAD_DIM = 128
FOX_HEADS = 16
FOX_KV_HEADS = 2
FOX_GROUP = FOX_HEADS // FOX_KV_HEADS
MLA_HEADS = 16
Q_LORA = 1024
KV_LORA = 512
NOPE_DIM = 128
ROPE_DIM = 64
V_DIM = 128
ROPE_BASE = 10000.0
D_FF = 4 * D_MODEL
EPS = 1e-6
FOX_W = FOX_HEADS * HEAD_DIM
MLA_W = MLA_HEADS * V_DIM
PAGE_SIZE = 128
FOX_SCALE = HEAD_DIM ** -0.5
MLA_SCALE = (NOPE_DIM + ROPE_DIM) ** -0.5

LANES = 128
VMEM_LIMIT_BYTES = 56 * 1024 * 1024

Z_QF = 0
Z_QLAT = FOX_W
Z_CKV = Z_QLAT + Q_LORA
Z_KF = Z_CKV + KV_LORA
Z_VF = Z_KF + FOX_KV_HEADS * HEAD_DIM
Z_KR = Z_VF + FOX_KV_HEADS * HEAD_DIM
Z_F = Z_KR + LANES
Z_USED = Z_F + LANES
Z_W = 4608

NEG_INF = float("-inf")


def _cparams(*sem):
    return pltpu.CompilerParams(dimension_semantics=sem, vmem_limit_bytes=VMEM_LIMIT_BYTES)


def _mm_kernel(a_ref, b_ref, o_ref, *, relu2):
    acc = jnp.dot(a_ref[...], b_ref[...], preferred_element_type=jnp.float32)
    if relu2:
        acc = jnp.square(jnp.maximum(acc, 0.0))
    o_ref[...] = acc.astype(o_ref.dtype)


def _mm_acc_kernel(a_ref, b_ref, o_ref, acc_ref):
    k = pl.program_id(2)

    @pl.when(k == 0)
    def _():
        acc_ref[...] = jnp.zeros_like(acc_ref)

    acc_ref[...] += jnp.dot(a_ref[...], b_ref[...], preferred_element_type=jnp.float32)

    @pl.when(k == pl.num_programs(2) - 1)
    def _():
        o_ref[...] = acc_ref[...].astype(o_ref.dtype)


def _matmul(a, b, out_dtype, tm, tn, *, tk=None, relu2=False, name):
    m, kdim = a.shape
    _, n = b.shape
    assert m % tm == 0 and n % tn == 0
    if tk is None:
        return pl.pallas_call(
            functools.partial(_mm_kernel, relu2=relu2),
            grid=(m // tm, n // tn),
            in_specs=[pl.BlockSpec((tm, kdim), lambda i, j: (i, 0)),
                      pl.BlockSpec((kdim, tn), lambda i, j: (0, j))],
            out_specs=pl.BlockSpec((tm, tn), lambda i, j: (i, j)),
            out_shape=jax.ShapeDtypeStruct((m, n), out_dtype),
            compiler_params=_cparams("parallel", "parallel"),
            name=name,
        )(a, b)
    assert kdim % tk == 0 and not relu2
    return pl.pallas_call(
        _mm_acc_kernel,
        grid=(m // tm, n // tn, kdim // tk),
        in_specs=[pl.BlockSpec((tm, tk), lambda i, j, k: (i, k)),
                  pl.BlockSpec((tk, tn), lambda i, j, k: (k, j))],
        out_specs=pl.BlockSpec((tm, tn), lambda i, j, k: (i, j)),
        out_shape=jax.ShapeDtypeStruct((m, n), out_dtype),
        scratch_shapes=[pltpu.VMEM((tm, tn), jnp.float32)],
        compiler_params=_cparams("parallel", "parallel", "arbitrary"),
        name=name,
    )(a, b)


def _rms(x, g):
    return x * lax.rsqrt(jnp.mean(x * x, axis=-1, keepdims=True) + EPS) * g


def _rms_cast_kernel(x_ref, g_ref, o_ref):
    o_ref[...] = _rms(x_ref[...], g_ref[...]).astype(o_ref.dtype)


def _rms_cast(x, g, tr, name):
    m, d = x.shape
    return pl.pallas_call(
        _rms_cast_kernel,
        grid=(m // tr,),
        in_specs=[pl.BlockSpec((tr, d), lambda i: (i, 0)),
                  pl.BlockSpec((1, d), lambda i: (0, 0))],
        out_specs=pl.BlockSpec((tr, d), lambda i: (i, 0)),
        out_shape=jax.ShapeDtypeStruct((m, d), jnp.bfloat16),
        compiler_params=_cparams("parallel"),
        name=name,
    )(x, g.reshape(1, d))


def _rope128(x, cos_t, sa_t, sb_t):
    n = x.shape[-1]
    half = ROPE_DIM // 2
    return x * cos_t + pltpu.roll(x, n - half, 1) * sa_t + pltpu.roll(x, half, 1) * sb_t


def _post_in_kernel(qlat_ref, ckv_ref, kr_ref, f_ref, cos_ref, sa_ref, sb_ref,
                    gq_ref, gkv_ref, bf_ref,
                    qlat_o, ckv_o, ckvb_o, kr_o, krb_o, logf_o):
    qlat_o[...] = _rms(qlat_ref[...], gq_ref[...]).astype(qlat_o.dtype)
    ckv = _rms(ckv_ref[...], gkv_ref[...])
    ckv_o[...] = ckv
    ckvb_o[...] = ckv.astype(ckvb_o.dtype)
    kr = _rope128(kr_ref[...], cos_ref[...], sa_ref[...], sb_ref[...])
    kr_o[...] = kr
    krb_o[...] = kr.astype(krb_o.dtype)
    x = f_ref[...] + bf_ref[...]
    logf_o[...] = jnp.minimum(x, 0.0) - jnp.log1p(jnp.exp(-jnp.abs(x)))


def _post_in(z, tabs, tab_blocks, g_q_lat, g_kv_lat, b_f_pad, tr, name):
    m = z.shape[0]
    cos_t, sa_t, sb_t = tabs
    tab_spec = pl.BlockSpec((tr, LANES), lambda i: (i % tab_blocks, 0))
    row = lambda w, off: pl.BlockSpec((tr, w), lambda i: (i, off // w))
    const = lambda w: pl.BlockSpec((1, w), lambda i: (0, 0))
    return pl.pallas_call(
        _post_in_kernel,
        grid=(m // tr,),
        in_specs=[row(Q_LORA, Z_QLAT), row(KV_LORA, Z_CKV), row(LANES, Z_KR), row(LANES, Z_F),
                  tab_spec, tab_spec, tab_spec, const(Q_LORA), const(KV_LORA), const(LANES)],
        out_specs=[pl.BlockSpec((tr, Q_LORA), lambda i: (i, 0)),
                   pl.BlockSpec((tr, KV_LORA), lambda i: (i, 0)),
                   pl.BlockSpec((tr, KV_LORA), lambda i: (i, 0)),
                   pl.BlockSpec((tr, LANES), lambda i: (i, 0)),
                   pl.BlockSpec((tr, LANES), lambda i: (i, 0)),
                   pl.BlockSpec((tr, LANES), lambda i: (i, 0))],
        out_shape=[jax.ShapeDtypeStruct((m, Q_LORA), jnp.bfloat16),
                   jax.ShapeDtypeStruct((m, KV_LORA), jnp.float32),
                   jax.ShapeDtypeStruct((m, KV_LORA), jnp.bfloat16),
                   jax.ShapeDtypeStruct((m, LANES), jnp.float32),
                   jax.ShapeDtypeStruct((m, LANES), jnp.bfloat16),
                   jax.ShapeDtypeStruct((m, LANES), jnp.float32)],
        compiler_params=_cparams("parallel"),
        name=name,
    )(z, z, z, z, cos_t, sa_t, sb_t, g_q_lat.reshape(1, -1), g_kv_lat.reshape(1, -1), b_f_pad)


def _post_q_kernel(nope_ref, rope_ref, cos_ref, sa_ref, sb_ref, nope_o, rope_o):
    nope_o[...] = nope_ref[...].astype(nope_o.dtype)
    cos_t, sa_t, sb_t = cos_ref[...], sa_ref[...], sb_ref[...]
    for h in range(MLA_HEADS):
        sl = slice(h * LANES, (h + 1) * LANES)
        rope_o[:, sl] = _rope128(rope_ref[:, sl], cos_t, sa_t, sb_t).astype(rope_o.dtype)


def _post_q(q, tabs, tab_blocks, tr, name):
    m = q.shape[0]
    w = MLA_HEADS * LANES
    tab_spec = pl.BlockSpec((tr, LANES), lambda i: (i % tab_blocks, 0))
    return pl.pallas_call(
        _post_q_kernel,
        grid=(m // tr,),
        in_specs=[pl.BlockSpec((tr, w), lambda i: (i, 0)), pl.BlockSpec((tr, w), lambda i: (i, 1)),
                  tab_spec, tab_spec, tab_spec],
        out_specs=[pl.BlockSpec((tr, w), lambda i: (i, 0)), pl.BlockSpec((tr, w), lambda i: (i, 0))],
        out_shape=[jax.ShapeDtypeStruct((m, w), jnp.bfloat16), jax.ShapeDtypeStruct((m, w), jnp.bfloat16)],
        compiler_params=_cparams("parallel"),
        name=name,
    )(q, q, *tabs)


def _fin1_kernel(of_ref, om_ref, gf_ref, gm_ref, o_ref):
    o_ref[:, :FOX_W] = _rms(of_ref[...], gf_ref[...]).astype(o_ref.dtype)
    o_ref[:, FOX_W:] = _rms(om_ref[...], gm_ref[...]).astype(o_ref.dtype)


def _fin1(o_f, o_m, g_f, g_m, tr, name):
    m = o_f.shape[0]
    return pl.pallas_call(
        _fin1_kernel,
        grid=(m // tr,),
        in_specs=[pl.BlockSpec((tr, FOX_W), lambda i: (i, 0)), pl.BlockSpec((tr, MLA_W), lambda i: (i, 0)),
                  pl.BlockSpec((1, FOX_W), lambda i: (0, 0)), pl.BlockSpec((1, MLA_W), lambda i: (0, 0))],
        out_specs=pl.BlockSpec((tr, FOX_W + MLA_W), lambda i: (i, 0)),
        out_shape=jax.ShapeDtypeStruct((m, FOX_W + MLA_W), jnp.bfloat16),
        compiler_params=_cparams("parallel"),
        name=name,
    )(o_f, o_m, g_f.reshape(1, -1), g_m.reshape(1, -1))


def _fin2_kernel(x_ref, y_ref, g1_ref, g2_ref, x1_o, h_o):
    x1 = x_ref[...] + _rms(y_ref[...], g1_ref[...])
    x1_o[...] = x1
    h_o[...] = _rms(x1, g2_ref[...]).astype(h_o.dtype)


def _fin2(x, y, g1, g2, tr, name):
    m, d = x.shape
    rs = pl.BlockSpec((tr, d), lambda i: (i, 0))
    gs = pl.BlockSpec((1, d), lambda i: (0, 0))
    return pl.pallas_call(
        _fin2_kernel,
        grid=(m // tr,),
        in_specs=[rs, rs, gs, gs],
        out_specs=[rs, rs],
        out_shape=[jax.ShapeDtypeStruct((m, d), jnp.float32), jax.ShapeDtypeStruct((m, d), jnp.bfloat16)],
        compiler_params=_cparams("parallel"),
        name=name,
    )(x, y, g1.reshape(1, d), g2.reshape(1, d))


def _fin3_kernel(x_ref, y_ref, g_ref, o_ref):
    o_ref[...] = x_ref[...] + _rms(y_ref[...], g_ref[...])


def _fin3(x, y, g, tr, name):
    m, d = x.shape
    rs = pl.BlockSpec((tr, d), lambda i: (i, 0))
    return pl.pallas_call(
        _fin3_kernel,
        grid=(m // tr,),
        in_specs=[rs, rs, pl.BlockSpec((1, d), lambda i: (0, 0))],
        out_specs=rs,
        out_shape=jax.ShapeDtypeStruct((m, d), jnp.float32),
        compiler_params=_cparams("parallel"),
        name=name,
    )(x, y, g.reshape(1, d))


def _split3(x):
    hi = x.astype(jnp.bfloat16)
    r1 = x - hi.astype(jnp.float32)
    mid = r1.astype(jnp.bfloat16)
    lo = (r1 - mid.astype(jnp.float32)).astype(jnp.bfloat16)
    return hi, mid, lo


def _dot_f32_by01(x, b01):
    hi, mid, lo = _split3(x)
    d = lambda a: jnp.dot(a, b01, preferred_element_type=jnp.float32)
    return d(hi) + d(mid) + d(lo)


def _dot01_by_f32(a01, x):
    hi, mid, lo = _split3(x)
    d = lambda b: jnp.dot(a01, b, preferred_element_type=jnp.float32)
    return d(hi) + d(mid) + d(lo)


T_PAD = 2304
ATT_BLK = 256


def _cum_prompt_kernel(lf_ref, c_ref, x_scr, c_scr):
    t = lf_ref.shape[0]
    x_scr[...] = jnp.zeros_like(x_scr)
    x_scr[0:t, :] = lf_ref[...]
    r = lax.broadcasted_iota(jnp.int32, (LANES, LANES), 0)
    c = lax.broadcasted_iota(jnp.int32, (LANES, LANES), 1)
    tri = (c <= r).astype(jnp.bfloat16)
    ones = jnp.ones((LANES, LANES), jnp.bfloat16)

    def body(k, carry):
        rows = pl.ds(pl.multiple_of(k * LANES, LANES), LANES)
        x = x_scr[rows, :]
        c_scr[rows, :] = _dot01_by_f32(tri, x) + carry
        return carry + _dot01_by_f32(ones, x)

    lax.fori_loop(0, x_scr.shape[0] // LANES, body, jnp.zeros((LANES, LANES), jnp.float32))
    c_ref[...] = c_scr[0:t, :]


def _cum_prompt(logf128, batch, t):
    return pl.pallas_call(
        _cum_prompt_kernel,
        grid=(batch,),
        in_specs=[pl.BlockSpec((t, LANES), lambda b: (b, 0))],
        out_specs=pl.BlockSpec((t, LANES), lambda b: (b, 0)),
        out_shape=jax.ShapeDtypeStruct((batch * t, LANES), jnp.float32),
        scratch_shapes=[pltpu.VMEM((T_PAD, LANES), jnp.float32), pltpu.VMEM((T_PAD, LANES), jnp.float32)],
        compiler_params=_cparams("parallel"),
        name="cum_prompt",
    )(logf128)


def _flash_prompt_kernel(*refs, fox, scale):
    if fox:
        q_ref, k_ref, v_ref, cq_ref, ck_ref, o_ref, q_scr, k_scr, v_scr, o_scr = refs
    else:
        qn_ref, qr_ref, kn_ref, kr_ref, v_ref, o_ref, q_scr, k_scr, v_scr, o_scr = refs
    t = o_ref.shape[0]
    dk = q_scr.shape[1]
    tail = pl.ds(t, T_PAD - t)
    q_scr[tail, :] = jnp.zeros((T_PAD - t, dk), q_scr.dtype)
    k_scr[tail, :] = jnp.zeros((T_PAD - t, dk), k_scr.dtype)
    v_scr[tail, :] = jnp.zeros((T_PAD - t, v_scr.shape[1]), v_scr.dtype)
    if fox:
        q_scr[0:t, :] = q_ref[...].astype(q_scr.dtype)
        k_scr[0:t, :] = k_ref[...].astype(k_scr.dtype)
    else:
        q_scr[0:t, 0:NOPE_DIM] = qn_ref[...]
        q_scr[0:t, NOPE_DIM:] = qr_ref[...]
        k_scr[0:t, 0:NOPE_DIM] = kn_ref[...]
        k_scr[0:t, NOPE_DIM:] = kr_ref[...]
    v_scr[0:t, :] = v_ref[...].astype(v_scr.dtype)

    def scores(qi, kj):
        qrows = pl.ds(pl.multiple_of(qi * ATT_BLK, ATT_BLK), ATT_BLK)
        krows = pl.ds(pl.multiple_of(kj * ATT_BLK, ATT_BLK), ATT_BLK)
        s = lax.dot_general(q_scr[qrows, :], k_scr[krows, :], (((1,), (1,)), ((), ())),
                            preferred_element_type=jnp.float32) * scale
        if fox:
            s = s + (cq_ref[0, 0, qrows, :][:, 0:1] - ck_ref[0, 0, :, krows])
        return s, krows

    def update(s, krows, carry):
        m, l, acc = carry
        m_new = jnp.maximum(m, jnp.max(s, axis=-1, keepdims=True))
        alpha = jnp.exp(m - m_new)
        p = jnp.exp(s - m_new)
        l = alpha * l + jnp.sum(p, axis=-1, keepdims=True)
        acc = alpha * acc + jnp.dot(p.astype(v_scr.dtype), v_scr[krows, :], preferred_element_type=jnp.float32)
        return m_new, l, acc

    def q_block(qi, _):
        def off_diag(kj, carry):
            s, krows = scores(qi, kj)
            return update(s, krows, carry)

        init = (jnp.full((ATT_BLK, 1), NEG_INF, jnp.float32), jnp.zeros((ATT_BLK, 1), jnp.float32),
                jnp.zeros((ATT_BLK, v_scr.shape[1]), jnp.float32))
        carry = lax.fori_loop(0, qi, off_diag, init)
        s, krows = scores(qi, qi)
        r = lax.broadcasted_iota(jnp.int32, (ATT_BLK, ATT_BLK), 0)
        c = lax.broadcasted_iota(jnp.int32, (ATT_BLK, ATT_BLK), 1)
        s = jnp.where(c <= r, s, NEG_INF)
        _, l, acc = update(s, krows, carry)
        o_scr[pl.ds(pl.multiple_of(qi * ATT_BLK, ATT_BLK), ATT_BLK), :] = acc / l
        return 0

    lax.fori_loop(0, T_PAD // ATT_BLK, q_block, 0)
    o_ref[...] = o_scr[0:t, :]


def _flash_prompt_fox(z, cq, ck, batch, t):
    hb = lambda off: (lambda b, h: (b, off // HEAD_DIM + h))
    kvb = lambda off: (lambda b, h: (b, off // HEAD_DIM + h // FOX_GROUP))
    blk = (t, HEAD_DIM)
    return pl.pallas_call(
        functools.partial(_flash_prompt_kernel, fox=True, scale=FOX_SCALE),
        grid=(batch, FOX_HEADS),
        in_specs=[pl.BlockSpec(blk, hb(Z_QF)), pl.BlockSpec(blk, kvb(Z_KF)), pl.BlockSpec(blk, kvb(Z_VF)),
                  pl.BlockSpec((1, 1, T_PAD, LANES), lambda b, h: (b, h, 0, 0)),
                  pl.BlockSpec((1, 1, 1, T_PAD), lambda b, h: (b, h, 0, 0))],
        out_specs=pl.BlockSpec(blk, lambda b, h: (b, h)),
        out_shape=jax.ShapeDtypeStruct((batch * t, FOX_W), jnp.float32),
        scratch_shapes=[pltpu.VMEM((T_PAD, HEAD_DIM), jnp.bfloat16), pltpu.VMEM((T_PAD, HEAD_DIM), jnp.bfloat16),
                        pltpu.VMEM((T_PAD, HEAD_DIM), jnp.bfloat16), pltpu.VMEM((T_PAD, HEAD_DIM), jnp.float32)],
        compiler_params=_cparams("parallel", "parallel"),
        name="flash_prompt_fox",
    )(z, z, z, cq, ck)


def _flash_prompt_mla(qn, qr, kv_up, krb, batch, t):
    blk = (t, LANES)
    dk = NOPE_DIM + LANES
    return pl.pallas_call(
        functools.partial(_flash_prompt_kernel, fox=False, scale=MLA_SCALE),
        grid=(batch, MLA_HEADS),
        in_specs=[pl.BlockSpec(blk, lambda b, h: (b, h)), pl.BlockSpec(blk, lambda b, h: (b, h)),
                  pl.BlockSpec(blk, lambda b, h: (b, h)), pl.BlockSpec(blk, lambda b, h: (b, 0)),
                  pl.BlockSpec(blk, lambda b, h: (b, MLA_HEADS + h))],
        out_specs=pl.BlockSpec(blk, lambda b, h: (b, h)),
        out_shape=jax.ShapeDtypeStruct((batch * t, MLA_W), jnp.float32),
        scratch_shapes=[pltpu.VMEM((T_PAD, dk), jnp.bfloat16), pltpu.VMEM((T_PAD, dk), jnp.bfloat16),
                        pltpu.VMEM((T_PAD, V_DIM), jnp.bfloat16), pltpu.VMEM((T_PAD, V_DIM), jnp.float32)],
        compiler_params=_cparams("parallel", "parallel"),
        name="flash_prompt_mla",
    )(qn, qr, kv_up, krb, kv_up)


def _head_mm_kernel(a_ref, b_ref, o_ref):
    o_ref[...] = jnp.dot(a_ref[...], b_ref[0], preferred_element_type=jnp.float32).astype(o_ref.dtype)


def _q_absorb(qn_s, w_uk_t):
    m = qn_s.shape[0]
    return pl.pallas_call(
        _head_mm_kernel,
        grid=(MLA_HEADS,),
        in_specs=[pl.BlockSpec((m, NOPE_DIM), lambda h: (0, h)),
                  pl.BlockSpec((1, NOPE_DIM, KV_LORA), lambda h: (h, 0, 0))],
        out_specs=pl.BlockSpec((m, KV_LORA), lambda h: (0, h)),
        out_shape=jax.ShapeDtypeStruct((m, MLA_HEADS * KV_LORA), jnp.bfloat16),
        compiler_params=_cparams("parallel"),
        name="q_absorb",
    )(qn_s, w_uk_t)


def _o_unabsorb(o_lat, w_uv):
    m = o_lat.shape[0]
    return pl.pallas_call(
        _head_mm_kernel,
        grid=(MLA_HEADS,),
        in_specs=[pl.BlockSpec((m, KV_LORA), lambda h: (0, h)),
                  pl.BlockSpec((1, KV_LORA, V_DIM), lambda h: (h, 0, 0))],
        out_specs=pl.BlockSpec((m, V_DIM), lambda h: (0, h)),
        out_shape=jax.ShapeDtypeStruct((m, MLA_W), jnp.float32),
        compiler_params=_cparams("parallel"),
        name="o_unabsorb",
    )(o_lat, w_uv)


PAGES_PER_STEP = 8
Q_ROWS = 64


def _decode_kernel(pt_ref, *refs, n_steps):
    del pt_ref
    npg = PAGES_PER_STEP
    page_refs = [refs[5 * p:5 * p + 5] for p in range(npg)]
    new_refs = refs[5 * npg:5 * npg + 5]
    qf_ref, qa_ref, qr_ref = refs[5 * npg + 5:5 * npg + 8]
    of_ref, ol_ref = refs[5 * npg + 8:5 * npg + 10]
    mf_scr, lf_scr, af_scr, mm_scr, lm_scr, am_scr, run_scr, qt_scr = refs[5 * npg + 10:]
    step = pl.program_id(1)
    half = Q_ROWS // FOX_KV_HEADS
    n_tok = half // FOX_GROUP

    r = lax.broadcasted_iota(jnp.int32, (LANES, LANES), 0)
    c = lax.broadcasted_iota(jnp.int32, (LANES, LANES), 1)
    sfx = (r > c).astype(jnp.bfloat16)
    ones = jnp.ones((LANES, LANES), jnp.bfloat16)
    nt = (((1,), (1,)), ((), ()))

    def process(pages, is_new):
        n = len(pages)
        g_list = [None] * n
        run = run_scr[...]
        for p in reversed(range(n)):
            lf = pages[p][4][0]
            g_list[p] = run + _dot_f32_by01(lf, sfx)
            run = run + _dot_f32_by01(lf, ones)
        run_scr[...] = run
        if is_new:
            for kvh in range(FOX_KV_HEADS):
                for i in range(n_tok):
                    col = g_list[0][kvh * FOX_GROUP:(kvh + 1) * FOX_GROUP, i:i + 1]
                    row0 = kvh * half + i * FOX_GROUP
                    qt_scr[row0:row0 + FOX_GROUP, :] = jnp.broadcast_to(-col, (FOX_GROUP, LANES))
        slot = lax.broadcasted_iota(jnp.int32, (1, LANES), 1)

        for kvh in range(FOX_KV_HEADS):
            rows = slice(kvh * half, (kvh + 1) * half)
            q = qf_ref[0, rows, :]
            s_parts = []
            v_parts = []
            for p in range(n):
                k_ref, v_ref = pages[p][0], pages[p][1]
                k = k_ref[0, pl.ds(kvh, PAGE_SIZE, stride=FOX_KV_HEADS), :].astype(jnp.bfloat16)
                v_parts.append(v_ref[0, pl.ds(kvh, PAGE_SIZE, stride=FOX_KV_HEADS), :].astype(jnp.bfloat16))
                s = lax.dot_general(q, k, nt, preferred_element_type=jnp.float32) * FOX_SCALE
                g = g_list[p][kvh * FOX_GROUP:(kvh + 1) * FOX_GROUP, :]
                s = s + jnp.concatenate([g] * n_tok, axis=0) + qt_scr[rows, :]
                if is_new:
                    tok = lax.broadcasted_iota(jnp.int32, (half, LANES), 0) // FOX_GROUP
                    s = jnp.where(slot <= tok, s, NEG_INF)
                s_parts.append(s)
            s_all = s_parts[0] if n == 1 else jnp.concatenate(s_parts, axis=-1)
            m_old = mf_scr[rows, :]
            m_new = jnp.maximum(m_old, jnp.max(s_all, axis=-1, keepdims=True))
            alpha = jnp.exp(m_old - m_new)
            pr = jnp.exp(s_all - m_new[:, 0:1])
            lf_scr[rows, :] = alpha * lf_scr[rows, :] + jnp.sum(pr, axis=-1, keepdims=True)
            mf_scr[rows, :] = m_new
            pv = jnp.zeros((half, HEAD_DIM), jnp.float32)
            prb = pr.astype(jnp.bfloat16)
            for p in range(n):
                pv = pv + jnp.dot(prb[:, p * LANES:(p + 1) * LANES], v_parts[p], preferred_element_type=jnp.float32)
            af_scr[rows, :] = alpha * af_scr[rows, :] + pv

        qa = qa_ref[0]
        qr = qr_ref[0]
        s_parts = []
        c_parts = []
        for p in range(n):
            ckv = pages[p][2][0].astype(jnp.bfloat16)
            kr = pages[p][3][0].astype(jnp.bfloat16)
            c_parts.append(ckv)
            s = (lax.dot_general(qa, ckv, nt, preferred_element_type=jnp.float32)
                 + lax.dot_general(qr, kr, nt, preferred_element_type=jnp.float32)) * MLA_SCALE
            if is_new:
                tok = lax.broadcasted_iota(jnp.int32, (Q_ROWS, LANES), 0) % n_tok
                s = jnp.where(slot <= tok, s, NEG_INF)
            s_parts.append(s)
        s_all = s_parts[0] if n == 1 else jnp.concatenate(s_parts, axis=-1)
        m_old = mm_scr[...]
        m_new = jnp.maximum(m_old, jnp.max(s_all, axis=-1, keepdims=True))
        alpha = jnp.exp(m_old - m_new)
        pr = jnp.exp(s_all - m_new[:, 0:1])
        lm_scr[...] = alpha * lm_scr[...] + jnp.sum(pr, axis=-1, keepdims=True)
        mm_scr[...] = m_new
        pv = jnp.zeros((Q_ROWS, KV_LORA), jnp.float32)
        prb = pr.astype(jnp.bfloat16)
        for p in range(n):
            pv = pv + jnp.dot(prb[:, p * LANES:(p + 1) * LANES], c_parts[p], preferred_element_type=jnp.float32)
        am_scr[...] = alpha[:, 0:1] * am_scr[...] + pv

    @pl.when(step == 0)
    def _():
        mf_scr[...] = jnp.full_like(mf_scr, NEG_INF)
        lf_scr[...] = jnp.zeros_like(lf_scr)
        af_scr[...] = jnp.zeros_like(af_scr)
        mm_scr[...] = jnp.full_like(mm_scr, NEG_INF)
        lm_scr[...] = jnp.zeros_like(lm_scr)
        am_scr[...] = jnp.zeros_like(am_scr)
        run_scr[...] = jnp.zeros_like(run_scr)
        process([new_refs], True)

    process(page_refs, False)

    @pl.when(step == n_steps - 1)
    def _():
        of_ref[0] = af_scr[...] / lf_scr[...]
        ol_ref[0] = (am_scr[...] / lm_scr[:, 0:1]).astype(ol_ref.dtype)


def _decode(page_table, caches, news, qf, qa, qr):
    n_seq, n_pages = page_table.shape
    npg = PAGES_PER_STEP
    n_steps = n_pages // npg
    widths = [c.shape[1:] for c in caches]

    def page_spec(shape, p):
        return pl.BlockSpec((1,) + tuple(shape),
                            lambda b, s, pt: (pt[b, (n_steps - 1 - s) * npg + p], 0, 0))

    in_specs = []
    operands = []
    for p in range(npg):
        for cch, shape in zip(caches, widths):
            in_specs.append(page_spec(shape, p))
            operands.append(cch)
    for nw in news:
        in_specs.append(pl.BlockSpec((1,) + tuple(nw.shape[1:]), lambda b, s, pt: (b, 0, 0)))
        operands.append(nw)
    for qq in (qf, qa, qr):
        in_specs.append(pl.BlockSpec((1,) + tuple(qq.shape[1:]), lambda b, s, pt: (b, 0, 0)))
        operands.append(qq)
    grid_spec = pltpu.PrefetchScalarGridSpec(
        num_scalar_prefetch=1,
        grid=(n_seq, n_steps),
        in_specs=in_specs,
        out_specs=[pl.BlockSpec((1, Q_ROWS, HEAD_DIM), lambda b, s, pt: (b, 0, 0)),
                   pl.BlockSpec((1, Q_ROWS, KV_LORA), lambda b, s, pt: (b, 0, 0))],
        scratch_shapes=[pltpu.VMEM((Q_ROWS, LANES), jnp.float32), pltpu.VMEM((Q_ROWS, LANES), jnp.float32),
                        pltpu.VMEM((Q_ROWS, HEAD_DIM), jnp.float32),
                        pltpu.VMEM((Q_ROWS, LANES), jnp.float32), pltpu.VMEM((Q_ROWS, LANES), jnp.float32),
                        pltpu.VMEM((Q_ROWS, KV_LORA), jnp.float32),
                        pltpu.VMEM((FOX_HEADS, LANES), jnp.float32), pltpu.VMEM((Q_ROWS, LANES), jnp.float32)],
    )
    return pl.pallas_call(
        functools.partial(_decode_kernel, n_steps=n_steps),
        grid_spec=grid_spec,
        out_shape=[jax.ShapeDtypeStruct((n_seq, Q_ROWS, HEAD_DIM), jnp.float32),
                   jax.ShapeDtypeStruct((n_seq, Q_ROWS, KV_LORA), jnp.bfloat16)],
        compiler_params=_cparams("parallel", "arbitrary"),
        name="decode_attention",
    )(page_table, *operands)


def _rope_tables(pos):
    half = ROPE_DIM // 2
    inv = ROPE_BASE ** (-jnp.arange(half, dtype=jnp.float32) / half)
    ang = pos.astype(jnp.float32)[:, None] * inv[None, :]
    cos, sin = jnp.cos(ang), jnp.sin(ang)
    z = jnp.zeros_like(cos)
    pad = jnp.zeros((pos.shape[0], LANES - ROPE_DIM), jnp.float32)
    cos_t = jnp.concatenate([cos, cos, pad], axis=1)
    sa_t = jnp.concatenate([-sin, z, pad], axis=1)
    sb_t = jnp.concatenate([z, sin, pad], axis=1)
    return cos_t, sa_t, sb_t


def _prep_weights(w_in, b_f, w_q_up, w_uk, w_uv, w_o, w_up, w_down):
    bf = jnp.bfloat16
    offs = [0]
    for s in (FOX_W, FOX_KV_HEADS * HEAD_DIM, FOX_KV_HEADS * HEAD_DIM, FOX_HEADS, Q_LORA, KV_LORA, ROPE_DIM):
        offs.append(offs[-1] + s)
    q_f, k_f, v_f, f_l, q_lat, c_kv, k_r = [w_in[:, offs[i]:offs[i + 1]] for i in range(7)]
    zc = lambda n: jnp.zeros((D_MODEL, n), w_in.dtype)
    w_in_p = jnp.concatenate([q_f, q_lat, c_kv, k_f, v_f, k_r, zc(LANES - ROPE_DIM), f_l, zc(LANES - FOX_HEADS),
                              zc(Z_W - Z_USED)], axis=1).astype(bf)
    b_f_pad = jnp.concatenate([b_f, jnp.zeros((LANES - FOX_HEADS,), b_f.dtype)]).reshape(1, LANES)
    wq = w_q_up.reshape(Q_LORA, MLA_HEADS, NOPE_DIM + ROPE_DIM)
    wq_nope = wq[:, :, :NOPE_DIM].reshape(Q_LORA, MLA_HEADS * NOPE_DIM)
    wq_rope = jnp.concatenate([wq[:, :, NOPE_DIM:], jnp.zeros((Q_LORA, MLA_HEADS, LANES - ROPE_DIM), wq.dtype)],
                              axis=2).reshape(Q_LORA, MLA_HEADS * LANES)
    wq_p = jnp.concatenate([wq_nope, wq_rope], axis=1).astype(bf)
    wk_all = w_uk.transpose(1, 0, 2).reshape(KV_LORA, MLA_HEADS * NOPE_DIM)
    wv_all = w_uv.transpose(1, 0, 2).reshape(KV_LORA, MLA_W)
    w_kv = jnp.concatenate([wk_all, wv_all], axis=1).astype(bf)
    w_uk_t = w_uk.transpose(0, 2, 1).astype(bf)
    return w_in_p, b_f_pad, wq_p, w_kv, w_uk_t, w_uv.astype(bf), w_o.astype(bf), w_up.astype(bf), w_down.astype(bf)


def _dense_tail(x, o_f, o_m, g, wts, tm, tr, tag):
    w_o, w_up, w_down = wts
    o = _fin1(o_f, o_m, g["fox_out"], g["mla_out"], tr, "fin1_" + tag)
    y = _matmul(o, w_o, jnp.float32, tm, 512, name="mm_o_" + tag)
    x1, h2 = _fin2(x, y, g["post_mix"], g["pre_mlp"], tr, "fin2_" + tag)
    u = _matmul(h2, w_up, jnp.bfloat16, tm, 512, relu2=True, name="mm_up_" + tag)
    d = _matmul(u, w_down, jnp.float32, tm, 1024, tk=2048, name="mm_down_" + tag)
    return _fin3(x1, d, g["post_mlp"], tr, "fin3_" + tag)


def kernel(x_prompt, x_sample, cache_fox_k, cache_fox_v, cache_fox_logf, cache_mla_ckv, cache_mla_krope,
           page_table, meta_tokens, g_pre_mix, w_in, b_f, g_q_lat, w_q_up, w_uk, g_kv_lat, w_uv, g_fox_out,
           g_mla_out, w_o, g_post_mix, g_pre_mlp, w_up, w_down, g_post_mlp):
    depth = w_in.shape[0]
    assert depth == 1
    batch, seq, _ = x_prompt.shape
    t = seq + N_META
    n_seq, dec_seq, _ = x_sample.shape
    n_past = page_table.shape[1] * PAGE_SIZE
    assert dec_seq * FOX_HEADS == Q_ROWS

    (w_in_p, b_f_pad, wq_p, w_kv, w_uk_t, w_uv_b, w_o_b, w_up_b, w_down_b) = _prep_weights(
        w_in[0], b_f[0], w_q_up[0], w_uk[0], w_uv[0], w_o[0], w_up[0], w_down[0])
    g = {"fox_out": g_fox_out[0], "mla_out": g_mla_out[0], "post_mix": g_post_mix[0],
         "pre_mlp": g_pre_mlp[0], "post_mlp": g_post_mlp[0]}
    tail_w = (w_o_b, w_up_b, w_down_b)

    meta = jnp.broadcast_to(meta_tokens[None].astype(x_prompt.dtype), (batch, N_META, D_MODEL))
    xp = jnp.concatenate([meta, x_prompt], axis=1).reshape(batch * t, D_MODEL)
    mp = batch * t
    tm_p, tr_wide, tr_p = 1376, 192, 688
    tabs_p = _rope_tables(jnp.arange(t))
    h = _rms_cast(xp, g_pre_mix[0], tr_wide, "rms_in_p")
    z = _matmul(h, w_in_p, jnp.float32, tm_p, 512, name="mm_in_p")
    qlat, ckv, ckvb, kr, krb, logf = _post_in(z, tabs_p, t // tr_p, g_q_lat[0], g_kv_lat[0], b_f_pad, tr_p,
                                              "post_in_p")
    q = _matmul(qlat, wq_p, jnp.float32, tm_p, 1024, name="mm_qup_p")
    qn, qr = _post_q(q, tabs_p, t // tr_p, tr_p, "post_q_p")
    kv_up = _matmul(ckvb, w_kv, jnp.bfloat16, tm_p, 1024, name="mm_kvup_p")
    cum = _cum_prompt(logf, batch, t)
    cum_t = cum[:, :FOX_HEADS].reshape(batch, t, FOX_HEADS).transpose(0, 2, 1)
    cum_t = jnp.pad(cum_t, ((0, 0), (0, 0), (0, T_PAD - t)))
    cq = jnp.broadcast_to(cum_t[..., None], (batch, FOX_HEADS, T_PAD, LANES))
    ck = cum_t[:, :, None, :]
    o_f = _flash_prompt_fox(z, cq, ck, batch, t)
    o_m = _flash_prompt_mla(qn, qr, kv_up, krb, batch, t)
    yp = _dense_tail(xp, o_f, o_m, g, tail_w, tm_p, tr_wide, "p")

    k_f_p = z[:, Z_KF:Z_KF + FOX_KV_HEADS * HEAD_DIM].reshape(1, batch, t, FOX_KV_HEADS, HEAD_DIM)
    v_f_p = z[:, Z_VF:Z_VF + FOX_KV_HEADS * HEAD_DIM].reshape(1, batch, t, FOX_KV_HEADS, HEAD_DIM)
    logf_p = logf[:, :FOX_HEADS].reshape(1, batch, t, FOX_HEADS)
    ckv_p = ckv.reshape(1, batch, t, KV_LORA)
    kr_p = kr[:, :ROPE_DIM].reshape(1, batch, t, ROPE_DIM)
    y_prompt = yp.reshape(batch, t, D_MODEL)[:, N_META:]

    ms = n_seq * dec_seq
    xs = x_sample.reshape(ms, D_MODEL)
    tabs_s = _rope_tables(n_past + jnp.arange(dec_seq))
    tabs_s = tuple(jnp.tile(tb, (ms // dec_seq, 1)) for tb in tabs_s)
    hs = _rms_cast(xs, g_pre_mix[0], 128, "rms_in_s")
    zs = _matmul(hs, w_in_p, jnp.float32, ms, 768, name="mm_in_s")
    qlat_s, ckv_s, _, kr_s, _, logf_s = _post_in(zs, tabs_s, 1, g_q_lat[0], g_kv_lat[0], b_f_pad, ms, "post_in_s")
    q_s = _matmul(qlat_s, wq_p, jnp.float32, ms, 1024, name="mm_qup_s")
    qn_s, qr_s = _post_q(q_s, tabs_s, 1, ms, "post_q_s")
    qa_s = _q_absorb(qn_s, w_uk_t)

    kf_s = zs[:, Z_KF:Z_KF + FOX_KV_HEADS * HEAD_DIM]
    vf_s = zs[:, Z_VF:Z_VF + FOX_KV_HEADS * HEAD_DIM]
    qf_d = zs[:, Z_QF:Z_QF + FOX_W].astype(jnp.bfloat16).reshape(n_seq, dec_seq, FOX_KV_HEADS, FOX_GROUP, HEAD_DIM)
    qf_d = qf_d.transpose(0, 2, 1, 3, 4).reshape(n_seq, Q_ROWS, HEAD_DIM)
    qa_d = qa_s.reshape(n_seq, dec_seq, MLA_HEADS, KV_LORA).transpose(0, 2, 1, 3).reshape(n_seq, Q_ROWS, KV_LORA)
    qr_d = qr_s.reshape(n_seq, dec_seq, MLA_HEADS, LANES)[..., :ROPE_DIM].transpose(0, 2, 1, 3)
    qr_d = qr_d.reshape(n_seq, Q_ROWS, ROPE_DIM)
    pad_slots = lambda a, rows: jnp.pad(a, ((0, 0), (0, rows - a.shape[1]), (0, 0)))
    k_new = pad_slots(kf_s.reshape(n_seq, dec_seq * FOX_KV_HEADS, HEAD_DIM), PAGE_SIZE * FOX_KV_HEADS)
    v_new = pad_slots(vf_s.reshape(n_seq, dec_seq * FOX_KV_HEADS, HEAD_DIM), PAGE_SIZE * FOX_KV_HEADS)
    ckv_new = pad_slots(ckv_s.reshape(n_seq, dec_seq, KV_LORA), PAGE_SIZE)
    kr_new = pad_slots(kr_s[:, :ROPE_DIM].reshape(n_seq, dec_seq, ROPE_DIM), PAGE_SIZE)
    lf_new = jnp.pad(logf_s[:, :FOX_HEADS].reshape(n_seq, dec_seq, FOX_HEADS).transpose(0, 2, 1),
                     ((0, 0), (0, 0), (0, PAGE_SIZE - dec_seq)))
    n_pool = cache_fox_k.shape[1]
    caches = (cache_fox_k[0].reshape(n_pool, PAGE_SIZE * FOX_KV_HEADS, HEAD_DIM),
              cache_fox_v[0].reshape(n_pool, PAGE_SIZE * FOX_KV_HEADS, HEAD_DIM),
              cache_mla_ckv[0], cache_mla_krope[0],
              cache_fox_logf[0].transpose(0, 2, 1))
    of_d, ol_d = _decode(page_table, caches, (k_new, v_new, ckv_new, kr_new, lf_new), qf_d, qa_d, qr_d)
    o_f_s = of_d.reshape(n_seq, FOX_KV_HEADS, dec_seq, FOX_GROUP, HEAD_DIM).transpose(0, 2, 1, 3, 4)
    o_f_s = o_f_s.reshape(ms, FOX_W)
    ol_s = ol_d.reshape(n_seq, MLA_HEADS, dec_seq, KV_LORA).transpose(0, 2, 1, 3).reshape(ms, MLA_HEADS * KV_LORA)
    o_m_s = _o_unabsorb(ol_s, w_uv_b)
    ys = _dense_tail(xs, o_f_s, o_m_s, g, tail_w, ms, 128, "s")

    k_f_s = kf_s.reshape(1, n_seq, dec_seq, FOX_KV_HEADS, HEAD_DIM)
    v_f_s = vf_s.reshape(1, n_seq, dec_seq, FOX_KV_HEADS, HEAD_DIM)
    logf_s_o = logf_s[:, :FOX_HEADS].reshape(1, n_seq, dec_seq, FOX_HEADS)
    ckv_s_o = ckv_s.reshape(1, n_seq, dec_seq, KV_LORA)
    kr_s_o = kr_s[:, :ROPE_DIM].reshape(1, n_seq, dec_seq, ROPE_DIM)
    y_sample = ys.reshape(n_seq, dec_seq, D_MODEL)

    return (y_prompt, y_sample, k_f_p, v_f_p, logf_p, ckv_p, kr_p, k_f_s, v_f_s, logf_s_o, ckv_s_o, kr_s_o)
```

```python
import functools

import jax
import jax.numpy as jnp
from jax import lax
from jax.experimental import pallas as pl
from jax.experimental.pallas import tpu as pltpu

D_MODEL = 4096
N_META = 16
HEAD_DIM = 128
FOX_HEADS = 16
FOX_KV_HEADS = 2
FOX_GROUP = FOX_HEADS // FOX_KV_HEADS
MLA_HEADS = 16
Q_LORA = 1024
KV_LORA = 512
NOPE_DIM = 128
ROPE_DIM = 64
V_DIM = 128
ROPE_BASE = 10000.0
D_FF = 4 * D_MODEL
EPS = 1e-6
FOX_W = FOX_HEADS * HEAD_DIM
MLA_W = MLA_HEADS * V_DIM
PAGE_SIZE = 128
FOX_SCALE = HEAD_DIM ** -0.5
MLA_SCALE = (NOPE_DIM + ROPE_DIM) ** -0.5

LANES = 128
VMEM_LIMIT_BYTES = 56 * 1024 * 1024

Z_QF = 0
Z_QLAT = FOX_W
Z_CKV = Z_QLAT + Q_LORA
Z_KF = Z_CKV + KV_LORA
Z_VF = Z_KF + FOX_KV_HEADS * HEAD_DIM
Z_KR = Z_VF + FOX_KV_HEADS * HEAD_DIM
Z_F = Z_KR + LANES
Z_USED = Z_F + LANES
Z_W = 4608

NEG_INF = float("-inf")
NT_DIMS = (((1,), (1,)), ((), ()))


def _cparams(*sem):
    return pltpu.CompilerParams(dimension_semantics=sem, vmem_limit_bytes=VMEM_LIMIT_BYTES)


def _mm_kernel(a_ref, b_ref, o_ref, *, relu2):
    acc = jnp.dot(a_ref[...], b_ref[...], preferred_element_type=jnp.float32)
    if relu2:
        acc = jnp.square(jnp.maximum(acc, 0.0))
    o_ref[...] = acc.astype(o_ref.dtype)


def _mm_acc_kernel(a_ref, b_ref, o_ref, acc_ref):
    k = pl.program_id(2)

    @pl.when(k == 0)
    def _():
        acc_ref[...] = jnp.zeros_like(acc_ref)

    acc_ref[...] += jnp.dot(a_ref[...], b_ref[...], preferred_element_type=jnp.float32)

    @pl.when(k == pl.num_programs(2) - 1)
    def _():
        o_ref[...] = acc_ref[...].astype(o_ref.dtype)


def _matmul(a, b, out_dtype, tm, tn, *, tk=None, relu2=False, name):
    m, kdim = a.shape
    _, n = b.shape
    assert m % tm == 0 and n % tn == 0
    if tk is None:
        return pl.pallas_call(
            functools.partial(_mm_kernel, relu2=relu2),
            grid=(m // tm, n // tn),
            in_specs=[pl.BlockSpec((tm, kdim), lambda i, j: (i, 0)),
                      pl.BlockSpec((kdim, tn), lambda i, j: (0, j))],
            out_specs=pl.BlockSpec((tm, tn), lambda i, j: (i, j)),
            out_shape=jax.ShapeDtypeStruct((m, n), out_dtype),
            compiler_params=_cparams("parallel", "parallel"),
            name=name,
        )(a, b)
    assert kdim % tk == 0 and not relu2
    return pl.pallas_call(
        _mm_acc_kernel,
        grid=(m // tm, n // tn, kdim // tk),
        in_specs=[pl.BlockSpec((tm, tk), lambda i, j, k: (i, k)),
                  pl.BlockSpec((tk, tn), lambda i, j, k: (k, j))],
        out_specs=pl.BlockSpec((tm, tn), lambda i, j, k: (i, j)),
        out_shape=jax.ShapeDtypeStruct((m, n), out_dtype),
        scratch_shapes=[pltpu.VMEM((tm, tn), jnp.float32)],
        compiler_params=_cparams("parallel", "parallel", "arbitrary"),
        name=name,
    )(a, b)


def _rms(x, g):
    return x * lax.rsqrt(jnp.mean(x * x, axis=-1, keepdims=True) + EPS) * g


def _rms_cast_kernel(x_ref, g_ref, o_ref):
    o_ref[...] = _rms(x_ref[...], g_ref[...]).astype(o_ref.dtype)


def _rms_cast(x, g, tr, name):
    m, d = x.shape
    return pl.pallas_call(
        _rms_cast_kernel,
        grid=(m // tr,),
        in_specs=[pl.BlockSpec((tr, d), lambda i: (i, 0)),
                  pl.BlockSpec((1, d), lambda i: (0, 0))],
        out_specs=pl.BlockSpec((tr, d), lambda i: (i, 0)),
        out_shape=jax.ShapeDtypeStruct((m, d), jnp.bfloat16),
        compiler_params=_cparams("parallel"),
        name=name,
    )(x, g.reshape(1, d))


def _rope128(x, cos_t, sa_t, sb_t):
    n = x.shape[-1]
    half = ROPE_DIM // 2
    return x * cos_t + pltpu.roll(x, n - half, 1) * sa_t + pltpu.roll(x, half, 1) * sb_t


def _post_in_kernel(qlat_ref, ckv_ref, kr_ref, f_ref, cos_ref, sa_ref, sb_ref,
                    gq_ref, gkv_ref, bf_ref,
                    qlat_o, ckv_o, ckvb_o, kr_o, krb_o, logf_o):
    qlat_o[...] = _rms(qlat_ref[...], gq_ref[...]).astype(qlat_o.dtype)
    ckv = _rms(ckv_ref[...], gkv_ref[...])
    ckv_o[...] = ckv
    ckvb_o[...] = ckv.astype(ckvb_o.dtype)
    kr = _rope128(kr_ref[...], cos_ref[...], sa_ref[...], sb_ref[...])
    kr_o[...] = kr
    krb_o[...] = kr.astype(krb_o.dtype)
    x = f_ref[...] + bf_ref[...]
    logf_o[...] = jnp.minimum(x, 0.0) - jnp.log1p(jnp.exp(-jnp.abs(x)))


def _post_in(z, tabs, tab_blocks, g_q_lat, g_kv_lat, b_f_pad, tr, name):
    m = z.shape[0]
    cos_t, sa_t, sb_t = tabs
    tab_spec = pl.BlockSpec((tr, LANES), lambda i: (i % tab_blocks, 0))
    row = lambda w, off: pl.BlockSpec((tr, w), lambda i: (i, off // w))
    const = lambda w: pl.BlockSpec((1, w), lambda i: (0, 0))
    return pl.pallas_call(
        _post_in_kernel,
        grid=(m // tr,),
        in_specs=[row(Q_LORA, Z_QLAT), row(KV_LORA, Z_CKV), row(LANES, Z_KR), row(LANES, Z_F),
                  tab_spec, tab_spec, tab_spec, const(Q_LORA), const(KV_LORA), const(LANES)],
        out_specs=[pl.BlockSpec((tr, Q_LORA), lambda i: (i, 0)),
                   pl.BlockSpec((tr, KV_LORA), lambda i: (i, 0)),
                   pl.BlockSpec((tr, KV_LORA), lambda i: (i, 0)),
                   pl.BlockSpec((tr, LANES), lambda i: (i, 0)),
                   pl.BlockSpec((tr, LANES), lambda i: (i, 0)),
                   pl.BlockSpec((tr, LANES), lambda i: (i, 0))],
        out_shape=[jax.ShapeDtypeStruct((m, Q_LORA), jnp.bfloat16),
                   jax.ShapeDtypeStruct((m, KV_LORA), jnp.float32),
                   jax.ShapeDtypeStruct((m, KV_LORA), jnp.bfloat16),
                   jax.ShapeDtypeStruct((m, LANES), jnp.float32),
                   jax.ShapeDtypeStruct((m, LANES), jnp.bfloat16),
                   jax.ShapeDtypeStruct((m, LANES), jnp.float32)],
        compiler_params=_cparams("parallel"),
        name=name,
    )(z, z, z, z, cos_t, sa_t, sb_t, g_q_lat.reshape(1, -1), g_kv_lat.reshape(1, -1), b_f_pad)


def _post_q_kernel(nope_ref, rope_ref, cos_ref, sa_ref, sb_ref, nope_o, rope_o, *, nope_scale):
    nope = nope_ref[...]
    nope_o[...] = (nope if nope_scale == 1.0 else nope * nope_scale).astype(nope_o.dtype)
    cos_t, sa_t, sb_t = cos_ref[...], sa_ref[...], sb_ref[...]
    for h in range(MLA_HEADS):
        sl = slice(h * LANES, (h + 1) * LANES)
        rope_o[:, sl] = (_rope128(rope_ref[:, sl], cos_t, sa_t, sb_t) * MLA_SCALE).astype(rope_o.dtype)


def _post_q(q, tabs, tab_blocks, tr, nope_scale, name):
    m = q.shape[0]
    w = MLA_HEADS * LANES
    tab_spec = pl.BlockSpec((tr, LANES), lambda i: (i % tab_blocks, 0))
    return pl.pallas_call(
        functools.partial(_post_q_kernel, nope_scale=nope_scale),
        grid=(m // tr,),
        in_specs=[pl.BlockSpec((tr, w), lambda i: (i, 0)), pl.BlockSpec((tr, w), lambda i: (i, 1)),
                  tab_spec, tab_spec, tab_spec],
        out_specs=[pl.BlockSpec((tr, w), lambda i: (i, 0)), pl.BlockSpec((tr, w), lambda i: (i, 0))],
        out_shape=[jax.ShapeDtypeStruct((m, w), jnp.bfloat16), jax.ShapeDtypeStruct((m, w), jnp.bfloat16)],
        compiler_params=_cparams("parallel"),
        name=name,
    )(q, q, *tabs)


def _fin1_kernel(of_ref, om_ref, gf_ref, gm_ref, o_ref):
    o_ref[:, :FOX_W] = _rms(of_ref[...], gf_ref[...]).astype(o_ref.dtype)
    o_ref[:, FOX_W:] = _rms(om_ref[...], gm_ref[...]).astype(o_ref.dtype)


def _fin1(o_f, o_m, g_f, g_m, tr, name):
    m = o_f.shape[0]
    return pl.pallas_call(
        _fin1_kernel,
        grid=(m // tr,),
        in_specs=[pl.BlockSpec((tr, FOX_W), lambda i: (i, 0)), pl.BlockSpec((tr, MLA_W), lambda i: (i, 0)),
                  pl.BlockSpec((1, FOX_W), lambda i: (0, 0)), pl.BlockSpec((1, MLA_W), lambda i: (0, 0))],
        out_specs=pl.BlockSpec((tr, FOX_W + MLA_W), lambda i: (i, 0)),
        out_shape=jax.ShapeDtypeStruct((m, FOX_W + MLA_W), jnp.bfloat16),
        compiler_params=_cparams("parallel"),
        name=name,
    )(o_f, o_m, g_f.reshape(1, -1), g_m.reshape(1, -1))


def _fin2_kernel(x_ref, y_ref, g1_ref, g2_ref, x1_o, h_o):
    x1 = x_ref[...] + _rms(y_ref[...], g1_ref[...])
    x1_o[...] = x1
    h_o[...] = _rms(x1, g2_ref[...]).astype(h_o.dtype)


def _fin2(x, y, g1, g2, tr, name):
    m, d = x.shape
    rs = pl.BlockSpec((tr, d), lambda i: (i, 0))
    gs = pl.BlockSpec((1, d), lambda i: (0, 0))
    return pl.pallas_call(
        _fin2_kernel,
        grid=(m // tr,),
        in_specs=[rs, rs, gs, gs],
        out_specs=[rs, rs],
        out_shape=[jax.ShapeDtypeStruct((m, d), jnp.float32), jax.ShapeDtypeStruct((m, d), jnp.bfloat16)],
        compiler_params=_cparams("parallel"),
        name=name,
    )(x, y, g1.reshape(1, d), g2.reshape(1, d))


def _fin3_kernel(x_ref, y_ref, g_ref, o_ref):
    o_ref[...] = x_ref[...] + _rms(y_ref[...], g_ref[...])


def _fin3(x, y, g, tr, name):
    m, d = x.shape
    rs = pl.BlockSpec((tr, d), lambda i: (i, 0))
    return pl.pallas_call(
        _fin3_kernel,
        grid=(m // tr,),
        in_specs=[rs, rs, pl.BlockSpec((1, d), lambda i: (0, 0))],
        out_specs=rs,
        out_shape=jax.ShapeDtypeStruct((m, d), jnp.float32),
        compiler_params=_cparams("parallel"),
        name=name,
    )(x, y, g.reshape(1, d))


def _split3(x):
    hi = x.astype(jnp.bfloat16)
    r1 = x - hi.astype(jnp.float32)
    mid = r1.astype(jnp.bfloat16)
    lo = (r1 - mid.astype(jnp.float32)).astype(jnp.bfloat16)
    return hi, mid, lo


def _dot_f32_by01(x, b01):
    hi, mid, lo = _split3(x)
    d = lambda a: jnp.dot(a, b01, preferred_element_type=jnp.float32)
    return d(hi) + d(mid) + d(lo)


def _dot01_by_f32(a01, x):
    hi, mid, lo = _split3(x)
    d = lambda b: jnp.dot(a01, b, preferred_element_type=jnp.float32)
    return d(hi) + d(mid) + d(lo)


def _round_up(n, m):
    return -(-n // m) * m


def _cum_prompt_kernel(lf_ref, c_ref, x_scr, c_scr):
    t = lf_ref.shape[0]
    x_scr[...] = jnp.zeros_like(x_scr)
    x_scr[0:t, :] = lf_ref[...]
    r = lax.broadcasted_iota(jnp.int32, (LANES, LANES), 0)
    c = lax.broadcasted_iota(jnp.int32, (LANES, LANES), 1)
    tri = (c <= r).astype(jnp.bfloat16)
    ones = jnp.ones((LANES, LANES), jnp.bfloat16)

    def body(k, carry):
        rows = pl.ds(pl.multiple_of(k * LANES, LANES), LANES)
        x = x_scr[rows, :]
        c_scr[rows, :] = _dot01_by_f32(tri, x) + carry
        return carry + _dot01_by_f32(ones, x)

    lax.fori_loop(0, x_scr.shape[0] // LANES, body, jnp.zeros((LANES, LANES), jnp.float32))
    c_ref[...] = c_scr[0:t, :]


def _cum_prompt(logf128, batch, t):
    tp = _round_up(t, LANES)
    return pl.pallas_call(
        _cum_prompt_kernel,
        grid=(batch,),
        in_specs=[pl.BlockSpec((t, LANES), lambda b: (b, 0))],
        out_specs=pl.BlockSpec((t, LANES), lambda b: (b, 0)),
        out_shape=jax.ShapeDtypeStruct((batch * t, LANES), jnp.float32),
        scratch_shapes=[pltpu.VMEM((tp, LANES), jnp.float32), pltpu.VMEM((tp, LANES), jnp.float32)],
        compiler_params=_cparams("parallel"),
        name="cum_prompt",
    )(logf128)


AUG_W = 2 * HEAD_DIM
N_SPLIT = 3


def _fox_aug_kernel(q_ref, k_ref, c_ref, qa_o, ka_o):
    hi, mid, lo = _split3(c_ref[...])
    cat = jnp.concatenate([hi, mid, lo], axis=1)
    ncat = -cat
    r = lax.broadcasted_iota(jnp.int32, (N_SPLIT * LANES, LANES), 0)
    col = lax.broadcasted_iota(jnp.int32, (N_SPLIT * LANES, LANES), 1)
    piece, head = r // LANES, r % LANES
    lane = lax.broadcasted_iota(jnp.int32, (1, LANES), 1)
    for kvh in range(FOX_KV_HEADS):
        g_of = head - kvh * FOX_GROUP
        sel_k = ((g_of >= 0) & (g_of < FOX_GROUP) & (col == N_SPLIT + N_SPLIT * g_of + piece)).astype(jnp.bfloat16)
        ext_k = jnp.dot(ncat, sel_k, preferred_element_type=jnp.float32) + (lane < N_SPLIT).astype(jnp.float32)
        ka_o[:, kvh * AUG_W:kvh * AUG_W + HEAD_DIM] = k_ref[:, kvh * HEAD_DIM:(kvh + 1) * HEAD_DIM].astype(ka_o.dtype)
        ka_o[:, kvh * AUG_W + HEAD_DIM:(kvh + 1) * AUG_W] = ext_k.astype(ka_o.dtype)
    for h in range(FOX_HEADS):
        g = h % FOX_GROUP
        sel_q = ((head == h) & (col == piece)).astype(jnp.bfloat16)
        own = ((lane >= N_SPLIT * (g + 1)) & (lane < N_SPLIT * (g + 2))).astype(jnp.float32)
        ext_q = jnp.dot(cat, sel_q, preferred_element_type=jnp.float32) + own
        qa_o[:, h * AUG_W:h * AUG_W + HEAD_DIM] = (q_ref[:, h * HEAD_DIM:(h + 1) * HEAD_DIM] * FOX_SCALE).astype(qa_o.dtype)
        qa_o[:, h * AUG_W + HEAD_DIM:(h + 1) * AUG_W] = ext_q.astype(qa_o.dtype)


def _fox_aug(z, cum, tr):
    m = z.shape[0]
    kw = FOX_KV_HEADS * HEAD_DIM
    return pl.pallas_call(
        _fox_aug_kernel,
        grid=(m // tr,),
        in_specs=[pl.BlockSpec((tr, FOX_W), lambda i: (i, Z_QF // FOX_W)),
                  pl.BlockSpec((tr, kw), lambda i: (i, Z_KF // kw)),
                  pl.BlockSpec((tr, LANES), lambda i: (i, 0))],
        out_specs=[pl.BlockSpec((tr, FOX_HEADS * AUG_W), lambda i: (i, 0)),
                   pl.BlockSpec((tr, FOX_KV_HEADS * AUG_W), lambda i: (i, 0))],
        out_shape=[jax.ShapeDtypeStruct((m, FOX_HEADS * AUG_W), jnp.bfloat16),
                   jax.ShapeDtypeStruct((m, FOX_KV_HEADS * AUG_W), jnp.bfloat16)],
        compiler_params=_cparams("parallel"),
        name="fox_aug",
    )(z, z, cum)


ATT_BLK = 256


def _flash_kernel(*refs, n_q, k_shared, v_shared, hs):
    n_k = len(k_shared)
    q_refs = refs[:n_q]
    k_refs = refs[n_q:n_q + n_k]
    v_ref = refs[n_q + n_k]
    o_ref = refs[n_q + n_k + 1]
    k_scr, v_scr, q_scr, m_scr, l_scr, acc_scr = refs[n_q + n_k + 2:]
    t = o_ref.shape[0]
    _, t_pad, dk = k_scr.shape
    dv = v_scr.shape[2]
    n_full, tail = t // ATT_BLK, t % ATT_BLK
    per_head_k = k_scr.shape[0] > 1
    per_head_v = v_scr.shape[0] > 1

    for j in range(k_scr.shape[0]):
        k_scr[j, pl.ds(t, t_pad - t), :] = jnp.zeros((t_pad - t, dk), k_scr.dtype)
        off = 0
        for r, shared in zip(k_refs, k_shared):
            w = r.shape[1] if shared else r.shape[1] // hs
            src = r[...] if shared else r[:, j * w:(j + 1) * w]
            k_scr[j, 0:t, off:off + w] = src.astype(k_scr.dtype)
            off += w
    for j in range(v_scr.shape[0]):
        v_scr[j, pl.ds(t, t_pad - t), :] = jnp.zeros((t_pad - t, dv), v_scr.dtype)
        src = v_ref[...] if v_shared else v_ref[:, j * dv:(j + 1) * dv]
        v_scr[j, 0:t, :] = src.astype(v_scr.dtype)

    def q_block(rows, tq, n_off, n_masked):
        mrows = hs * tq
        for j in range(hs):
            off = 0
            for r in q_refs:
                w = r.shape[1] // hs
                q_scr[j * tq:(j + 1) * tq, off:off + w] = r[rows, j * w:(j + 1) * w]
                off += w
        m_scr[0:mrows, :] = jnp.full((mrows, LANES), NEG_INF, jnp.float32)
        l_scr[0:mrows, :] = jnp.zeros((mrows, LANES), jnp.float32)
        acc_scr[0:mrows, :] = jnp.zeros((mrows, dv), jnp.float32)

        def attend(kj, mask_shift):
            krows = pl.ds(pl.multiple_of(kj * ATT_BLK, ATT_BLK), ATT_BLK)
            for j in range(hs):
                sl = slice(j * tq, (j + 1) * tq)
                s = lax.dot_general(q_scr[sl, :], k_scr[j if per_head_k else 0, krows, :], NT_DIMS,
                                    preferred_element_type=jnp.float32)
                if mask_shift is not None:
                    r_in = lax.broadcasted_iota(jnp.int32, (tq, ATT_BLK), 0)
                    c_in = lax.broadcasted_iota(jnp.int32, (tq, ATT_BLK), 1) + mask_shift
                    s = jnp.where(c_in <= r_in, s, NEG_INF)
                s0, s1 = s[:, 0:LANES], s[:, LANES:]
                m_old = m_scr[sl, :]
                m_new = jnp.maximum(m_old, jnp.max(jnp.maximum(s0, s1), axis=-1, keepdims=True))
                alpha = jnp.exp(m_old - m_new)
                p0 = jnp.exp(s0 - m_new)
                p1 = jnp.exp(s1 - m_new)
                l_scr[sl, :] = alpha * l_scr[sl, :] + jnp.sum(p0 + p1, axis=-1, keepdims=True)
                m_scr[sl, :] = m_new
                p = jnp.concatenate([p0, p1], axis=1).astype(v_scr.dtype)
                acc_scr[sl, :] = alpha * acc_scr[sl, :] + jnp.dot(
                    p, v_scr[j if per_head_v else 0, krows, :], preferred_element_type=jnp.float32)

        def off_diag(kj, carry):
            attend(kj, None)
            return carry

        lax.fori_loop(0, n_off, off_diag, 0)
        for d in range(n_masked):
            attend(n_off + d, d * ATT_BLK)
        for j in range(hs):
            sl = slice(j * tq, (j + 1) * tq)
            o_ref[rows, j * dv:(j + 1) * dv] = (acc_scr[sl, :] / l_scr[sl, :]).astype(o_ref.dtype)

    def main(qi, carry):
        q_block(pl.ds(pl.multiple_of(qi * ATT_BLK, ATT_BLK), ATT_BLK), ATT_BLK, qi, 1)
        return carry

    lax.fori_loop(0, n_full - 1, main, 0)
    q_block(slice((n_full - 1) * ATT_BLK, t), ATT_BLK + tail, n_full - 1, 2)


def _flash(q_parts, k_parts, v_part, batch, t, n_heads, hs, name):
    assert t % ATT_BLK and (t % ATT_BLK) % 16 == 0 and n_heads % hs == 0
    t_pad = _round_up(t, ATT_BLK)
    dv = V_DIM
    in_specs, operands = [], []
    for arr in q_parts:
        w = arr.shape[1] // n_heads
        in_specs.append(pl.BlockSpec((t, hs * w), lambda b, g: (b, g)))
        operands.append(arr)

    def kv_spec(part):
        _, w, col0, share = part
        if share:
            assert col0 % w == 0 and share % hs == 0
            return pl.BlockSpec((t, w), lambda b, g: (b, col0 // w + (g * hs) // share)), True
        assert col0 % (hs * w) == 0
        return pl.BlockSpec((t, hs * w), lambda b, g: (b, col0 // (hs * w) + g)), False

    shared = []
    for part in list(k_parts) + [v_part]:
        spec, is_shared = kv_spec(part)
        in_specs.append(spec)
        operands.append(part[0])
        shared.append(is_shared)
    v_shared = shared.pop()
    dk = sum(p[1] for p in k_parts)
    mrows = hs * (ATT_BLK + t % ATT_BLK)
    scratch = [pltpu.VMEM((1 if all(shared) else hs, t_pad, dk), jnp.bfloat16),
               pltpu.VMEM((1 if v_shared else hs, t_pad, dv), jnp.bfloat16),
               pltpu.VMEM((mrows, dk), jnp.bfloat16), pltpu.VMEM((mrows, LANES), jnp.float32),
               pltpu.VMEM((mrows, LANES), jnp.float32), pltpu.VMEM((mrows, dv), jnp.float32)]
    return pl.pallas_call(
        functools.partial(_flash_kernel, n_q=len(q_parts), k_shared=tuple(shared), v_shared=v_shared, hs=hs),
        grid=(batch, n_heads // hs),
        in_specs=in_specs,
        out_specs=pl.BlockSpec((t, hs * dv), lambda b, g: (b, g)),
        out_shape=jax.ShapeDtypeStruct((batch * t, n_heads * dv), jnp.float32),
        scratch_shapes=scratch,
        compiler_params=_cparams("parallel", "parallel"),
        name=name,
    )(*operands)


def _head_mm_kernel(a_ref, b_ref, o_ref, *, scale):
    acc = jnp.dot(a_ref[...], b_ref[0], preferred_element_type=jnp.float32)
    if scale != 1.0:
        acc = acc * scale
    o_ref[...] = acc.astype(o_ref.dtype)


def _head_mm(a, w, out_dtype, tm, scale, name):
    m = a.shape[0]
    n_heads, kdim, n = w.shape
    return pl.pallas_call(
        functools.partial(_head_mm_kernel, scale=scale),
        grid=(n_heads, m // tm),
        in_specs=[pl.BlockSpec((tm, kdim), lambda h, i: (i, h)),
                  pl.BlockSpec((1, kdim, n), lambda h, i: (h, 0, 0))],
        out_specs=pl.BlockSpec((tm, n), lambda h, i: (i, h)),
        out_shape=jax.ShapeDtypeStruct((m, n_heads * n), out_dtype),
        compiler_params=_cparams("parallel", "parallel"),
        name=name,
    )(a, w)


PAGES_PER_STEP = 16
Q_ROWS = 64
N_CACHES = 5


def _page_copies(pt_ref, caches, bufs, sems, seq, step, slot, n_steps):
    kc, vc, cc, rc, lc = caches
    kbuf, vbuf, cbuf, rbuf, lbuf = bufs
    npg = PAGES_PER_STEP
    copies = []
    for p in range(npg):
        page = pt_ref[seq, (n_steps - 1 - step) * npg + p]
        rows2 = pl.ds(p * PAGE_SIZE * FOX_KV_HEADS, PAGE_SIZE * FOX_KV_HEADS)
        copies.append(pltpu.make_async_copy(kc.at[page], kbuf.at[slot, rows2, :], sems.at[slot, 0]))
        copies.append(pltpu.make_async_copy(vc.at[page], vbuf.at[slot, rows2, :], sems.at[slot, 1]))
        copies.append(pltpu.make_async_copy(cc.at[page], cbuf.at[slot, pl.ds(p * PAGE_SIZE, PAGE_SIZE), :],
                                            sems.at[slot, 2]))
        copies.append(pltpu.make_async_copy(rc.at[page], rbuf.at[slot, :, pl.ds(p * PAGE_SIZE, PAGE_SIZE)],
                                            sems.at[slot, 3]))
        copies.append(pltpu.make_async_copy(lc.at[page], lbuf.at[slot, pl.ds(p * FOX_HEADS, FOX_HEADS), :],
                                            sems.at[slot, 4]))
    return copies


def _decode_kernel(pt_ref, kc, vc, cc, rc, lc, kn_ref, vn_ref, cn_ref, rn_ref, ln_ref, qf_ref, qa_ref, qr_ref,
                   of_ref, ol_ref,
                   kbuf, vbuf, cbuf, rbuf, lbuf, sems, mf_scr, lf_scr, af_scr, mm_scr, lm_scr, am_scr, run_scr,
                   qt_scr, *, n_steps):
    caches = (kc, vc, cc, rc, lc)
    bufs = (kbuf, vbuf, cbuf, rbuf, lbuf)
    seq = pl.program_id(0)
    step = pl.program_id(1)
    n_seq = pl.num_programs(0)
    slot = step % 2
    half = Q_ROWS // FOX_KV_HEADS
    n_tok = half // FOX_GROUP

    @pl.when(jnp.logical_and(seq == 0, step == 0))
    def _():
        for cp in _page_copies(pt_ref, caches, bufs, sems, seq, step, slot, n_steps):
            cp.start()

    last = jnp.logical_and(seq == n_seq - 1, step == n_steps - 1)

    @pl.when(jnp.logical_not(last))
    def _():
        wrap = step == n_steps - 1
        nseq = jnp.where(wrap, seq + 1, seq)
        nstep = jnp.where(wrap, 0, step + 1)
        for cp in _page_copies(pt_ref, caches, bufs, sems, nseq, nstep, 1 - slot, n_steps):
            cp.start()

    r = lax.broadcasted_iota(jnp.int32, (LANES, 2 * LANES), 0)
    c = lax.broadcasted_iota(jnp.int32, (LANES, 2 * LANES), 1)
    sfx_tot = ((r > c) | (c >= LANES)).astype(jnp.bfloat16)

    def attend(n_pages, lf_all, k_of, v_of, ckv_all, kr_all, is_new):
        width = n_pages * LANES
        st = _dot_f32_by01(lf_all, sfx_tot)
        run = run_scr[...]
        g_pages = [None] * n_pages
        for p in reversed(range(n_pages)):
            rows = slice(p * FOX_HEADS, (p + 1) * FOX_HEADS)
            g_pages[p] = run + st[rows, 0:LANES]
            run = run + st[rows, LANES:]
        run_scr[...] = run
        if is_new:
            for kvh in range(FOX_KV_HEADS):
                for i in range(n_tok):
                    col = g_pages[0][kvh * FOX_GROUP:(kvh + 1) * FOX_GROUP, i:i + 1]
                    row0 = kvh * half + i * FOX_GROUP
                    qt_scr[row0:row0 + FOX_GROUP, :] = jnp.broadcast_to(-col, (FOX_GROUP, LANES))
        slot_idx = lax.broadcasted_iota(jnp.int32, (1, width), 1)

        for kvh in range(FOX_KV_HEADS):
            rows = slice(kvh * half, (kvh + 1) * half)
            k = k_of(kvh).astype(jnp.bfloat16)
            v = v_of(kvh).astype(jnp.bfloat16)
            s = lax.dot_general(qf_ref[0, rows, :], k, NT_DIMS, preferred_element_type=jnp.float32)
            hrows = slice(kvh * FOX_GROUP, (kvh + 1) * FOX_GROUP)
            g = g_pages[0][hrows, :] if n_pages == 1 else jnp.concatenate([gp[hrows, :] for gp in g_pages], axis=1)
            s = s + jnp.concatenate([g] * n_tok, axis=0) + qt_scr[rows, 0:1]
            if is_new:
                tok = lax.broadcasted_iota(jnp.int32, (half, width), 0) // FOX_GROUP
                s = jnp.where(slot_idx <= tok, s, NEG_INF)
            m_old = mf_scr[rows, :]
            m_new = jnp.maximum(m_old, jnp.max(s, axis=-1, keepdims=True))
            alpha = jnp.exp(m_old - m_new)
            pr = jnp.exp(s - m_new[:, 0:1])
            lf_scr[rows, :] = alpha * lf_scr[rows, :] + jnp.sum(pr, axis=-1, keepdims=True)
            mf_scr[rows, :] = m_new
            af_scr[rows, :] = alpha * af_scr[rows, :] + jnp.dot(pr.astype(jnp.bfloat16), v,
                                                               preferred_element_type=jnp.float32)

        ckv = ckv_all.astype(jnp.bfloat16)
        s = (lax.dot_general(qa_ref[0], ckv, NT_DIMS, preferred_element_type=jnp.float32)
             + jnp.dot(qr_ref[0], kr_all.astype(jnp.bfloat16), preferred_element_type=jnp.float32))
        if is_new:
            tok = lax.broadcasted_iota(jnp.int32, (Q_ROWS, width), 0) % n_tok
            s = jnp.where(slot_idx <= tok, s, NEG_INF)
        m_old = mm_scr[...]
        m_new = jnp.maximum(m_old, jnp.max(s, axis=-1, keepdims=True))
        alpha = jnp.exp(m_old - m_new)
        pr = jnp.exp(s - m_new[:, 0:1])
        lm_scr[...] = alpha * lm_scr[...] + jnp.sum(pr, axis=-1, keepdims=True)
        mm_scr[...] = m_new
        am_scr[...] = alpha[:, 0:1] * am_scr[...] + jnp.dot(pr.astype(jnp.bfloat16), ckv,
                                                           preferred_element_type=jnp.float32)

    @pl.when(step == 0)
    def _():
        mf_scr[...] = jnp.full_like(mf_scr, NEG_INF)
        lf_scr[...] = jnp.zeros_like(lf_scr)
        af_scr[...] = jnp.zeros_like(af_scr)
        mm_scr[...] = jnp.full_like(mm_scr, NEG_INF)
        lm_scr[...] = jnp.zeros_like(lm_scr)
        am_scr[...] = jnp.zeros_like(am_scr)
        run_scr[...] = jnp.zeros_like(run_scr)
        strided = lambda ref: (lambda kvh: ref[0, pl.ds(kvh, PAGE_SIZE, stride=FOX_KV_HEADS), :])
        attend(1, ln_ref[0], strided(kn_ref), strided(vn_ref), cn_ref[0], rn_ref[0], True)

    for cp in _page_copies(pt_ref, caches, bufs, sems, seq, step, slot, n_steps):
        cp.wait()

    npg = PAGES_PER_STEP
    strided_buf = lambda buf: (lambda kvh: buf[slot, pl.ds(kvh, npg * PAGE_SIZE, stride=FOX_KV_HEADS), :])
    attend(npg, lbuf[slot], strided_buf(kbuf), strided_buf(vbuf), cbuf[slot], rbuf[slot], False)

    @pl.when(step == n_steps - 1)
    def _():
        of_ref[0] = af_scr[...] / lf_scr[...]
        ol_ref[0] = (am_scr[...] / lm_scr[:, 0:1]).astype(ol_ref.dtype)


def _decode(page_table, caches, news, qf, qa, qr):
    n_seq, n_pages = page_table.shape
    npg = PAGES_PER_STEP
    n_steps = n_pages // npg
    assert n_pages % npg == 0 and n_steps % 2 == 0
    per_seq = lambda arr: pl.BlockSpec((1,) + tuple(arr.shape[1:]), lambda b, s, pt: (b, 0, 0))
    any_spec = pl.BlockSpec(memory_space=pl.ANY)
    in_specs = [any_spec] * N_CACHES + [per_seq(a) for a in news] + [per_seq(a) for a in (qf, qa, qr)]
    rows2 = npg * PAGE_SIZE * FOX_KV_HEADS
    grid_spec = pltpu.PrefetchScalarGridSpec(
        num_scalar_prefetch=1,
        grid=(n_seq, n_steps),
        in_specs=in_specs,
        out_specs=[pl.BlockSpec((1, Q_ROWS, HEAD_DIM), lambda b, s, pt: (b, 0, 0)),
                   pl.BlockSpec((1, Q_ROWS, KV_LORA), lambda b, s, pt: (b, 0, 0))],
        scratch_shapes=[pltpu.VMEM((2, rows2, HEAD_DIM), jnp.float32), pltpu.VMEM((2, rows2, HEAD_DIM), jnp.float32),
                        pltpu.VMEM((2, npg * PAGE_SIZE, KV_LORA), jnp.float32),
                        pltpu.VMEM((2, ROPE_DIM, npg * PAGE_SIZE), jnp.float32),
                        pltpu.VMEM((2, npg * FOX_HEADS, LANES), jnp.float32),
                        pltpu.SemaphoreType.DMA((2, N_CACHES)),
                        pltpu.VMEM((Q_ROWS, LANES), jnp.float32), pltpu.VMEM((Q_ROWS, LANES), jnp.float32),
                        pltpu.VMEM((Q_ROWS, HEAD_DIM), jnp.float32),
                        pltpu.VMEM((Q_ROWS, LANES), jnp.float32), pltpu.VMEM((Q_ROWS, LANES), jnp.float32),
                        pltpu.VMEM((Q_ROWS, KV_LORA), jnp.float32),
                        pltpu.VMEM((FOX_HEADS, LANES), jnp.float32), pltpu.VMEM((Q_ROWS, LANES), jnp.float32)],
    )
    return pl.pallas_call(
        functools.partial(_decode_kernel, n_steps=n_steps),
        grid_spec=grid_spec,
        out_shape=[jax.ShapeDtypeStruct((n_seq, Q_ROWS, HEAD_DIM), jnp.float32),
                   jax.ShapeDtypeStruct((n_seq, Q_ROWS, KV_LORA), jnp.bfloat16)],
        compiler_params=_cparams("arbitrary", "arbitrary"),
        name="decode_attention",
    )(page_table, *caches, *news, qf, qa, qr)


def _rope_tables(pos):
    half = ROPE_DIM // 2
    inv = ROPE_BASE ** (-jnp.arange(half, dtype=jnp.float32) / half)
    ang = pos.astype(jnp.float32)[:, None] * inv[None, :]
    cos, sin = jnp.cos(ang), jnp.sin(ang)
    z = jnp.zeros_like(cos)
    pad = jnp.zeros((pos.shape[0], LANES - ROPE_DIM), jnp.float32)
    cos_t = jnp.concatenate([cos, cos, pad], axis=1)
    sa_t = jnp.concatenate([-sin, z, pad], axis=1)
    sb_t = jnp.concatenate([z, sin, pad], axis=1)
    return cos_t, sa_t, sb_t


def _prep_weights(w_in, b_f, w_q_up, w_uk, w_uv, w_o, w_up, w_down):
    bf = jnp.bfloat16
    offs = [0]
    for s in (FOX_W, FOX_KV_HEADS * HEAD_DIM, FOX_KV_HEADS * HEAD_DIM, FOX_HEADS, Q_LORA, KV_LORA, ROPE_DIM):
        offs.append(offs[-1] + s)
    q_f, k_f, v_f, f_l, q_lat, c_kv, k_r = [w_in[:, offs[i]:offs[i + 1]] for i in range(7)]
    zc = lambda n: jnp.zeros((D_MODEL, n), w_in.dtype)
    w_in_p = jnp.concatenate([q_f, q_lat, c_kv, k_f, v_f, k_r, zc(LANES - ROPE_DIM), f_l, zc(LANES - FOX_HEADS),
                              zc(Z_W - Z_USED)], axis=1).astype(bf)
    b_f_pad = jnp.concatenate([b_f, jnp.zeros((LANES - FOX_HEADS,), b_f.dtype)]).reshape(1, LANES)
    wq = w_q_up.reshape(Q_LORA, MLA_HEADS, NOPE_DIM + ROPE_DIM)
    wq_nope = wq[:, :, :NOPE_DIM].reshape(Q_LORA, MLA_HEADS * NOPE_DIM)
    wq_rope = jnp.concatenate([wq[:, :, NOPE_DIM:], jnp.zeros((Q_LORA, MLA_HEADS, LANES - ROPE_DIM), wq.dtype)],
                              axis=2).reshape(Q_LORA, MLA_HEADS * LANES)
    wq_p = jnp.concatenate([wq_nope, wq_rope], axis=1).astype(bf)
    wk_all = w_uk.transpose(1, 0, 2).reshape(KV_LORA, MLA_HEADS * NOPE_DIM)
    wv_all = w_uv.transpose(1, 0, 2).reshape(KV_LORA, MLA_W)
    w_kv = jnp.concatenate([wk_all, wv_all], axis=1).astype(bf)
    w_uk_t = w_uk.transpose(0, 2, 1).astype(bf)
    return (w_in_p, b_f_pad, wq_p, w_kv, w_uk_t, w_uv.astype(bf), w_o.astype(bf), w_up.astype(bf),
            w_down.astype(bf))


def _dense_tail(x, o_f, o_m, g, wts, tm, tr, tag):
    w_o, w_up, w_down = wts
    o = _fin1(o_f, o_m, g["fox_out"], g["mla_out"], tr, "fin1_" + tag)
    y = _matmul(o, w_o, jnp.float32, tm, 512, name="mm_o_" + tag)
    x1, h2 = _fin2(x, y, g["post_mix"], g["pre_mlp"], tr, "fin2_" + tag)
    u = _matmul(h2, w_up, jnp.bfloat16, tm, 512, relu2=True, name="mm_up_" + tag)
    d = _matmul(u, w_down, jnp.float32, tm, 1024, tk=2048, name="mm_down_" + tag)
    return _fin3(x1, d, g["post_mlp"], tr, "fin3_" + tag)


def kernel(x_prompt, x_sample, cache_fox_k, cache_fox_v, cache_fox_logf, cache_mla_ckv, cache_mla_krope,
           page_table, meta_tokens, g_pre_mix, w_in, b_f, g_q_lat, w_q_up, w_uk, g_kv_lat, w_uv, g_fox_out,
           g_mla_out, w_o, g_post_mix, g_pre_mlp, w_up, w_down, g_post_mlp):
    depth = w_in.shape[0]
    assert depth == 1
    batch, seq, _ = x_prompt.shape
    t = seq + N_META
    n_seq, dec_seq, _ = x_sample.shape
    n_past = page_table.shape[1] * PAGE_SIZE
    assert dec_seq * FOX_HEADS == Q_ROWS

    (w_in_p, b_f_pad, wq_p, w_kv, w_uk_t, w_uv_b, w_o_b, w_up_b, w_down_b) = _prep_weights(
        w_in[0], b_f[0], w_q_up[0], w_uk[0], w_uv[0], w_o[0], w_up[0], w_down[0])
    g = {"fox_out": g_fox_out[0], "mla_out": g_mla_out[0], "post_mix": g_post_mix[0],
         "pre_mlp": g_pre_mlp[0], "post_mlp": g_post_mlp[0]}
    tail_w = (w_o_b, w_up_b, w_down_b)

    meta = jnp.broadcast_to(meta_tokens[None].astype(x_prompt.dtype), (batch, N_META, D_MODEL))
    xp = jnp.concatenate([meta, x_prompt], axis=1).reshape(batch * t, D_MODEL)
    tm_p, tr_wide, tr_p = 1376, 192, 688
    tabs_p = _rope_tables(jnp.arange(t))
    h = _rms_cast(xp, g_pre_mix[0], tr_wide, "rms_in_p")
    z = _matmul(h, w_in_p, jnp.float32, tm_p, 512, name="mm_in_p")
    qlat, ckv, ckvb, kr, krb, logf = _post_in(z, tabs_p, t // tr_p, g_q_lat[0], g_kv_lat[0], b_f_pad, tr_p,
                                              "post_in_p")
    q = _matmul(qlat, wq_p, jnp.float32, tm_p, 1024, name="mm_qup_p")
    qn, qr = _post_q(q, tabs_p, t // tr_p, tr_p, MLA_SCALE, "post_q_p")
    kv_up = _matmul(ckvb, w_kv, jnp.bfloat16, tm_p, 1024, name="mm_kvup_p")
    o_f, o_m = _prompt_attention(z, logf, qn, qr, kv_up, krb, batch, t, tr_p)
    yp = _dense_tail(xp, o_f, o_m, g, tail_w, tm_p, tr_wide, "p")

    k_f_p = z[:, Z_KF:Z_KF + FOX_KV_HEADS * HEAD_DIM].reshape(1, batch, t, FOX_KV_HEADS, HEAD_DIM)
    v_f_p = z[:, Z_VF:Z_VF + FOX_KV_HEADS * HEAD_DIM].reshape(1, batch, t, FOX_KV_HEADS, HEAD_DIM)
    logf_p = logf[:, :FOX_HEADS].reshape(1, batch, t, FOX_HEADS)
    ckv_p = ckv.reshape(1, batch, t, KV_LORA)
    kr_p = kr[:, :ROPE_DIM].reshape(1, batch, t, ROPE_DIM)
    y_prompt = yp.reshape(batch, t, D_MODEL)[:, N_META:]

    ms = n_seq * dec_seq
    xs = x_sample.reshape(ms, D_MODEL)
    tabs_s = _rope_tables(n_past + jnp.arange(dec_seq))
    tabs_s = tuple(jnp.tile(tb, (ms // dec_seq, 1)) for tb in tabs_s)
    hs = _rms_cast(xs, g_pre_mix[0], 128, "rms_in_s")
    zs = _matmul(hs, w_in_p, jnp.float32, ms, 512, name="mm_in_s")
    qlat_s, ckv_s, _, kr_s, _, logf_s = _post_in(zs, tabs_s, 1, g_q_lat[0], g_kv_lat[0], b_f_pad, ms, "post_in_s")
    q_s = _matmul(qlat_s, wq_p, jnp.float32, ms, 1024, name="mm_qup_s")
    qn_s, qr_s = _post_q(q_s, tabs_s, 1, ms, 1.0, "post_q_s")
    qa_s = _head_mm(qn_s, w_uk_t, jnp.bfloat16, ms, MLA_SCALE, "q_absorb_s")

    kf_s = zs[:, Z_KF:Z_KF + FOX_KV_HEADS * HEAD_DIM]
    vf_s = zs[:, Z_VF:Z_VF + FOX_KV_HEADS * HEAD_DIM]
    o_f_s, ol_s = _sample_attention(zs, qa_s, qr_s, ckv_s, kr_s, logf_s, cache_fox_k[0], cache_fox_v[0],
                                    cache_fox_logf[0], cache_mla_ckv[0], cache_mla_krope[0], page_table,
                                    n_seq, dec_seq)
    o_m_s = _head_mm(ol_s, w_uv_b, jnp.float32, ms, 1.0, "o_unabsorb_s")
    ys = _dense_tail(xs, o_f_s, o_m_s, g, tail_w, ms, 128, "s")

    k_f_s = kf_s.reshape(1, n_seq, dec_seq, FOX_KV_HEADS, HEAD_DIM)
    v_f_s = vf_s.reshape(1, n_seq, dec_seq, FOX_KV_HEADS, HEAD_DIM)
    logf_s_o = logf_s[:, :FOX_HEADS].reshape(1, n_seq, dec_seq, FOX_HEADS)
    ckv_s_o = ckv_s.reshape(1, n_seq, dec_seq, KV_LORA)
    kr_s_o = kr_s[:, :ROPE_DIM].reshape(1, n_seq, dec_seq, ROPE_DIM)
    y_sample = ys.reshape(n_seq, dec_seq, D_MODEL)

    return (y_prompt, y_sample, k_f_p, v_f_p, logf_p, ckv_p, kr_p, k_f_s, v_f_s, logf_s_o, ckv_s_o, kr_s_o)


def _prompt_attention(z, logf, qn, qr, kv_up, krb, batch, t, tr):
    cum = _cum_prompt(logf, batch, t)
    q_aug, k_aug = _fox_aug(z, cum, tr)
    o_f = _flash([q_aug], [(k_aug, AUG_W, 0, FOX_GROUP)], (z, HEAD_DIM, Z_VF, FOX_GROUP), batch, t,
                 FOX_HEADS, 4, "flash_prompt_fox")
    o_m = _flash([qn, qr], [(kv_up, NOPE_DIM, 0, 0), (krb, LANES, 0, MLA_HEADS)],
                 (kv_up, V_DIM, MLA_HEADS * NOPE_DIM, 0), batch, t, MLA_HEADS, 4, "flash_prompt_mla")
    return o_f, o_m


def _sample_attention(zs, qa_s, qr_s, ckv_s, kr_s, logf_s, cache_k, cache_v, cache_lf, cache_ckv, cache_kr,
                      page_table, n_seq, dec_seq):
    ms = n_seq * dec_seq
    kf_s = zs[:, Z_KF:Z_KF + FOX_KV_HEADS * HEAD_DIM]
    vf_s = zs[:, Z_VF:Z_VF + FOX_KV_HEADS * HEAD_DIM]
    qf_d = (zs[:, Z_QF:Z_QF + FOX_W] * FOX_SCALE).astype(jnp.bfloat16)
    qf_d = qf_d.reshape(n_seq, dec_seq, FOX_KV_HEADS, FOX_GROUP, HEAD_DIM)
    qf_d = qf_d.transpose(0, 2, 1, 3, 4).reshape(n_seq, Q_ROWS, HEAD_DIM)
    qa_d = qa_s.reshape(n_seq, dec_seq, MLA_HEADS, KV_LORA).transpose(0, 2, 1, 3).reshape(n_seq, Q_ROWS, KV_LORA)
    qr_d = qr_s.reshape(n_seq, dec_seq, MLA_HEADS, LANES)[..., :ROPE_DIM].transpose(0, 2, 1, 3)
    qr_d = qr_d.reshape(n_seq, Q_ROWS, ROPE_DIM)
    pad_slots = lambda a, rows: jnp.pad(a, ((0, 0), (0, rows - a.shape[1]), (0, 0)))
    pad_lanes = lambda a: jnp.pad(a, ((0, 0), (0, 0), (0, PAGE_SIZE - a.shape[2])))
    k_new = pad_slots(kf_s.reshape(n_seq, dec_seq * FOX_KV_HEADS, HEAD_DIM), PAGE_SIZE * FOX_KV_HEADS)
    v_new = pad_slots(vf_s.reshape(n_seq, dec_seq * FOX_KV_HEADS, HEAD_DIM), PAGE_SIZE * FOX_KV_HEADS)
    ckv_new = pad_slots(ckv_s.reshape(n_seq, dec_seq, KV_LORA), PAGE_SIZE)
    kr_new = pad_lanes(kr_s[:, :ROPE_DIM].reshape(n_seq, dec_seq, ROPE_DIM).transpose(0, 2, 1))
    lf_new = pad_lanes(logf_s[:, :FOX_HEADS].reshape(n_seq, dec_seq, FOX_HEADS).transpose(0, 2, 1))
    n_pool = cache_k.shape[0]
    caches = (cache_k.reshape(n_pool, PAGE_SIZE * FOX_KV_HEADS, HEAD_DIM),
              cache_v.reshape(n_pool, PAGE_SIZE * FOX_KV_HEADS, HEAD_DIM),
              cache_ckv, cache_kr.transpose(0, 2, 1), cache_lf.transpose(0, 2, 1))
    of_d, ol_d = _decode(page_table, caches, (k_new, v_new, ckv_new, kr_new, lf_new), qf_d, qa_d, qr_d)
    o_f_s = of_d.reshape(n_seq, FOX_KV_HEADS, dec_seq, FOX_GROUP, HEAD_DIM).transpose(0, 2, 1, 3, 4)
    o_f_s = o_f_s.reshape(ms, FOX_W)
    ol_s = ol_d.reshape(n_seq, MLA_HEADS, dec_seq, KV_LORA).transpose(0, 2, 1, 3).reshape(ms, MLA_HEADS * KV_LORA)
    return o_f_s, ol_s
```

```python
import functools

import jax
import jax.numpy as jnp
from jax import lax
from jax.experimental import pallas as pl
from jax.experimental.pallas import tpu as pltpu

D_MODEL = 4096
N_META = 16
HEAD_DIM = 128
FOX_HEADS = 16
FOX_KV_HEADS = 2
FOX_GROUP = FOX_HEADS // FOX_KV_HEADS
MLA_HEADS = 16
Q_LORA = 1024
KV_LORA = 512
NOPE_DIM = 128
ROPE_DIM = 64
V_DIM = 128
ROPE_BASE = 10000.0
D_FF = 4 * D_MODEL
EPS = 1e-6
FOX_W = FOX_HEADS * HEAD_DIM
MLA_W = MLA_HEADS * V_DIM
PAGE_SIZE = 128
FOX_SCALE = HEAD_DIM ** -0.5
MLA_SCALE = (NOPE_DIM + ROPE_DIM) ** -0.5

LANES = 128
VMEM_LIMIT_BYTES = 56 * 1024 * 1024

Z_QF = 0
Z_QLAT = FOX_W
Z_CKV = Z_QLAT + Q_LORA
Z_KF = Z_CKV + KV_LORA
Z_VF = Z_KF + FOX_KV_HEADS * HEAD_DIM
Z_KR = Z_VF + FOX_KV_HEADS * HEAD_DIM
Z_F = Z_KR + LANES
Z_USED = Z_F + LANES
Z_W = 4608

NEG_INF = float("-inf")
NT_DIMS = (((1,), (1,)), ((), ()))


def _cparams(*sem):
    return pltpu.CompilerParams(dimension_semantics=sem, vmem_limit_bytes=VMEM_LIMIT_BYTES)


def _mm_kernel(a_ref, b_ref, o_ref, *, relu2):
    acc = jnp.dot(a_ref[...], b_ref[...].astype(a_ref.dtype), preferred_element_type=jnp.float32)
    if relu2:
        acc = jnp.square(jnp.maximum(acc, 0.0))
    o_ref[...] = acc.astype(o_ref.dtype)


def _mm_acc_kernel(a_ref, b_ref, o_ref, acc_ref):
    k = pl.program_id(2)

    @pl.when(k == 0)
    def _():
        acc_ref[...] = jnp.zeros_like(acc_ref)

    acc_ref[...] += jnp.dot(a_ref[...], b_ref[...].astype(a_ref.dtype), preferred_element_type=jnp.float32)

    @pl.when(k == pl.num_programs(2) - 1)
    def _():
        o_ref[...] = acc_ref[...].astype(o_ref.dtype)


def _matmul(a, b, out_dtype, tm, tn, *, tk=None, relu2=False, name):
    m, kdim = a.shape
    _, n = b.shape
    assert m % tm == 0 and n % tn == 0
    if tk is None:
        return pl.pallas_call(
            functools.partial(_mm_kernel, relu2=relu2),
            grid=(m // tm, n // tn),
            in_specs=[pl.BlockSpec((tm, kdim), lambda i, j: (i, 0)),
                      pl.BlockSpec((kdim, tn), lambda i, j: (0, j))],
            out_specs=pl.BlockSpec((tm, tn), lambda i, j: (i, j)),
            out_shape=jax.ShapeDtypeStruct((m, n), out_dtype),
            compiler_params=_cparams("parallel", "parallel"),
            name=name,
        )(a, b)
    assert kdim % tk == 0 and not relu2
    return pl.pallas_call(
        _mm_acc_kernel,
        grid=(m // tm, n // tn, kdim // tk),
        in_specs=[pl.BlockSpec((tm, tk), lambda i, j, k: (i, k)),
                  pl.BlockSpec((tk, tn), lambda i, j, k: (k, j))],
        out_specs=pl.BlockSpec((tm, tn), lambda i, j, k: (i, j)),
        out_shape=jax.ShapeDtypeStruct((m, n), out_dtype),
        scratch_shapes=[pltpu.VMEM((tm, tn), jnp.float32)],
        compiler_params=_cparams("parallel", "parallel", "arbitrary"),
        name=name,
    )(a, b)


def _rms(x, g):
    return x * lax.rsqrt(jnp.mean(x * x, axis=-1, keepdims=True) + EPS) * g


def _rms_cast_kernel(x_ref, g_ref, o_ref):
    o_ref[...] = _rms(x_ref[...], g_ref[...]).astype(o_ref.dtype)


def _rms_cast(x, g, tr, name):
    m, d = x.shape
    return pl.pallas_call(
        _rms_cast_kernel,
        grid=(m // tr,),
        in_specs=[pl.BlockSpec((tr, d), lambda i: (i, 0)),
                  pl.BlockSpec((1, d), lambda i: (0, 0))],
        out_specs=pl.BlockSpec((tr, d), lambda i: (i, 0)),
        out_shape=jax.ShapeDtypeStruct((m, d), jnp.bfloat16),
        compiler_params=_cparams("parallel"),
        name=name,
    )(x, g.reshape(1, d))


def _rope128(x, cos_t, sa_t, sb_t):
    n = x.shape[-1]
    half = ROPE_DIM // 2
    return x * cos_t + pltpu.roll(x, n - half, 1) * sa_t + pltpu.roll(x, half, 1) * sb_t


def _post_in_kernel(qlat_ref, ckv_ref, kr_ref, f_ref, cos_ref, sa_ref, sb_ref,
                    gq_ref, gkv_ref, bf_ref,
                    qlat_o, ckv_o, ckvb_o, kr_o, krb_o, logf_o):
    qlat_o[...] = _rms(qlat_ref[...], gq_ref[...]).astype(qlat_o.dtype)
    ckv = _rms(ckv_ref[...], gkv_ref[...])
    ckv_o[...] = ckv
    ckvb_o[...] = ckv.astype(ckvb_o.dtype)
    kr = _rope128(kr_ref[...], cos_ref[...], sa_ref[...], sb_ref[...])
    kr_o[...] = kr
    krb_o[...] = kr.astype(krb_o.dtype)
    x = f_ref[...] + bf_ref[...]
    logf_o[...] = jnp.minimum(x, 0.0) - jnp.log1p(jnp.exp(-jnp.abs(x)))


def _post_in(z, tabs, tab_blocks, g_q_lat, g_kv_lat, b_f_pad, tr, name):
    m = z.shape[0]
    cos_t, sa_t, sb_t = tabs
    tab_spec = pl.BlockSpec((tr, LANES), lambda i: (i % tab_blocks, 0))
    row = lambda w, off: pl.BlockSpec((tr, w), lambda i: (i, off // w))
    const = lambda w: pl.BlockSpec((1, w), lambda i: (0, 0))
    return pl.pallas_call(
        _post_in_kernel,
        grid=(m // tr,),
        in_specs=[row(Q_LORA, Z_QLAT), row(KV_LORA, Z_CKV), row(LANES, Z_KR), row(LANES, Z_F),
                  tab_spec, tab_spec, tab_spec, const(Q_LORA), const(KV_LORA), const(LANES)],
        out_specs=[pl.BlockSpec((tr, Q_LORA), lambda i: (i, 0)),
                   pl.BlockSpec((tr, KV_LORA), lambda i: (i, 0)),
                   pl.BlockSpec((tr, KV_LORA), lambda i: (i, 0)),
                   pl.BlockSpec((tr, LANES), lambda i: (i, 0)),
                   pl.BlockSpec((tr, LANES), lambda i: (i, 0)),
                   pl.BlockSpec((tr, LANES), lambda i: (i, 0))],
        out_shape=[jax.ShapeDtypeStruct((m, Q_LORA), jnp.bfloat16),
                   jax.ShapeDtypeStruct((m, KV_LORA), jnp.float32),
                   jax.ShapeDtypeStruct((m, KV_LORA), jnp.bfloat16),
                   jax.ShapeDtypeStruct((m, LANES), jnp.float32),
                   jax.ShapeDtypeStruct((m, LANES), jnp.bfloat16),
                   jax.ShapeDtypeStruct((m, LANES), jnp.float32)],
        compiler_params=_cparams("parallel"),
        name=name,
    )(z, z, z, z, cos_t, sa_t, sb_t, g_q_lat.reshape(1, -1), g_kv_lat.reshape(1, -1), b_f_pad)


def _post_q_kernel(nope_ref, rope_ref, cos_ref, sa_ref, sb_ref, nope_o, rope_o, *, nope_scale):
    nope = nope_ref[...]
    nope_o[...] = (nope if nope_scale == 1.0 else nope * nope_scale).astype(nope_o.dtype)
    cos_t, sa_t, sb_t = cos_ref[...], sa_ref[...], sb_ref[...]
    for h in range(MLA_HEADS):
        sl = slice(h * LANES, (h + 1) * LANES)
        rope_o[:, sl] = (_rope128(rope_ref[:, sl], cos_t, sa_t, sb_t) * MLA_SCALE).astype(rope_o.dtype)


def _post_q(q, tabs, tab_blocks, tr, nope_scale, name):
    m = q.shape[0]
    w = MLA_HEADS * LANES
    tab_spec = pl.BlockSpec((tr, LANES), lambda i: (i % tab_blocks, 0))
    return pl.pallas_call(
        functools.partial(_post_q_kernel, nope_scale=nope_scale),
        grid=(m // tr,),
        in_specs=[pl.BlockSpec((tr, w), lambda i: (i, 0)), pl.BlockSpec((tr, w), lambda i: (i, 1)),
                  tab_spec, tab_spec, tab_spec],
        out_specs=[pl.BlockSpec((tr, w), lambda i: (i, 0)), pl.BlockSpec((tr, w), lambda i: (i, 0))],
        out_shape=[jax.ShapeDtypeStruct((m, w), jnp.bfloat16), jax.ShapeDtypeStruct((m, w), jnp.bfloat16)],
        compiler_params=_cparams("parallel"),
        name=name,
    )(q, q, *tabs)


def _fin1_kernel(of_ref, om_ref, gf_ref, gm_ref, o_ref):
    o_ref[:, :FOX_W] = _rms(of_ref[...], gf_ref[...]).astype(o_ref.dtype)
    o_ref[:, FOX_W:] = _rms(om_ref[...], gm_ref[...]).astype(o_ref.dtype)


def _fin1(o_f, o_m, g_f, g_m, tr, name):
    m = o_f.shape[0]
    return pl.pallas_call(
        _fin1_kernel,
        grid=(m // tr,),
        in_specs=[pl.BlockSpec((tr, FOX_W), lambda i: (i, 0)), pl.BlockSpec((tr, MLA_W), lambda i: (i, 0)),
                  pl.BlockSpec((1, FOX_W), lambda i: (0, 0)), pl.BlockSpec((1, MLA_W), lambda i: (0, 0))],
        out_specs=pl.BlockSpec((tr, FOX_W + MLA_W), lambda i: (i, 0)),
        out_shape=jax.ShapeDtypeStruct((m, FOX_W + MLA_W), jnp.bfloat16),
        compiler_params=_cparams("parallel"),
        name=name,
    )(o_f, o_m, g_f.reshape(1, -1), g_m.reshape(1, -1))


def _fin2_kernel(x_ref, y_ref, g1_ref, g2_ref, x1_o, h_o):
    x1 = x_ref[...] + _rms(y_ref[...], g1_ref[...])
    x1_o[...] = x1
    h_o[...] = _rms(x1, g2_ref[...]).astype(h_o.dtype)


def _fin2(x, y, g1, g2, tr, name):
    m, d = x.shape
    rs = pl.BlockSpec((tr, d), lambda i: (i, 0))
    gs = pl.BlockSpec((1, d), lambda i: (0, 0))
    return pl.pallas_call(
        _fin2_kernel,
        grid=(m // tr,),
        in_specs=[rs, rs, gs, gs],
        out_specs=[rs, rs],
        out_shape=[jax.ShapeDtypeStruct((m, d), jnp.float32), jax.ShapeDtypeStruct((m, d), jnp.bfloat16)],
        compiler_params=_cparams("parallel"),
        name=name,
    )(x, y, g1.reshape(1, d), g2.reshape(1, d))


def _fin3_kernel(x_ref, y_ref, g_ref, o_ref):
    o_ref[...] = x_ref[...] + _rms(y_ref[...], g_ref[...])


def _fin3(x, y, g, tr, name):
    m, d = x.shape
    rs = pl.BlockSpec((tr, d), lambda i: (i, 0))
    return pl.pallas_call(
        _fin3_kernel,
        grid=(m // tr,),
        in_specs=[rs, rs, pl.BlockSpec((1, d), lambda i: (0, 0))],
        out_specs=rs,
        out_shape=jax.ShapeDtypeStruct((m, d), jnp.float32),
        compiler_params=_cparams("parallel"),
        name=name,
    )(x, y, g.reshape(1, d))


def _split3(x):
    hi = x.astype(jnp.bfloat16)
    r1 = x - hi.astype(jnp.float32)
    mid = r1.astype(jnp.bfloat16)
    lo = (r1 - mid.astype(jnp.float32)).astype(jnp.bfloat16)
    return hi, mid, lo


def _dot_f32_by01(x, b01):
    hi, mid, lo = _split3(x)
    d = lambda a: jnp.dot(a, b01, preferred_element_type=jnp.float32)
    return d(hi) + d(mid) + d(lo)


def _dot01_by_f32(a01, x):
    hi, mid, lo = _split3(x)
    d = lambda b: jnp.dot(a01, b, preferred_element_type=jnp.float32)
    return d(hi) + d(mid) + d(lo)


def _round_up(n, m):
    return -(-n // m) * m


def _cum_prompt_kernel(lf_ref, c_ref, x_scr, c_scr):
    t = lf_ref.shape[0]
    x_scr[...] = jnp.zeros_like(x_scr)
    x_scr[0:t, :] = lf_ref[...]
    r = lax.broadcasted_iota(jnp.int32, (LANES, LANES), 0)
    c = lax.broadcasted_iota(jnp.int32, (LANES, LANES), 1)
    tri = (c <= r).astype(jnp.bfloat16)
    ones = jnp.ones((LANES, LANES), jnp.bfloat16)

    def body(k, carry):
        rows = pl.ds(pl.multiple_of(k * LANES, LANES), LANES)
        x = x_scr[rows, :]
        c_scr[rows, :] = _dot01_by_f32(tri, x) + carry
        return carry + _dot01_by_f32(ones, x)

    lax.fori_loop(0, x_scr.shape[0] // LANES, body, jnp.zeros((LANES, LANES), jnp.float32))
    c_ref[...] = c_scr[0:t, :]


def _cum_prompt(logf128, batch, t):
    tp = _round_up(t, LANES)
    return pl.pallas_call(
        _cum_prompt_kernel,
        grid=(batch,),
        in_specs=[pl.BlockSpec((t, LANES), lambda b: (b, 0))],
        out_specs=pl.BlockSpec((t, LANES), lambda b: (b, 0)),
        out_shape=jax.ShapeDtypeStruct((batch * t, LANES), jnp.float32),
        scratch_shapes=[pltpu.VMEM((tp, LANES), jnp.float32), pltpu.VMEM((tp, LANES), jnp.float32)],
        compiler_params=_cparams("parallel"),
        name="cum_prompt",
    )(logf128)


AUG_W = 2 * HEAD_DIM
N_SPLIT = 3


def _fox_aug_kernel(q_ref, k_ref, c_ref, qa_o, ka_o):
    hi, mid, lo = _split3(c_ref[...])
    cat = jnp.concatenate([hi, mid, lo], axis=1)
    ncat = -cat
    r = lax.broadcasted_iota(jnp.int32, (N_SPLIT * LANES, LANES), 0)
    col = lax.broadcasted_iota(jnp.int32, (N_SPLIT * LANES, LANES), 1)
    piece, head = r // LANES, r % LANES
    lane = lax.broadcasted_iota(jnp.int32, (1, LANES), 1)
    for kvh in range(FOX_KV_HEADS):
        g_of = head - kvh * FOX_GROUP
        sel_k = ((g_of >= 0) & (g_of < FOX_GROUP) & (col == N_SPLIT + N_SPLIT * g_of + piece)).astype(jnp.bfloat16)
        ext_k = jnp.dot(ncat, sel_k, preferred_element_type=jnp.float32) + (lane < N_SPLIT).astype(jnp.float32)
        ka_o[:, kvh * AUG_W:kvh * AUG_W + HEAD_DIM] = k_ref[:, kvh * HEAD_DIM:(kvh + 1) * HEAD_DIM].astype(ka_o.dtype)
        ka_o[:, kvh * AUG_W + HEAD_DIM:(kvh + 1) * AUG_W] = ext_k.astype(ka_o.dtype)
    for h in range(FOX_HEADS):
        g = h % FOX_GROUP
        sel_q = ((head == h) & (col == piece)).astype(jnp.bfloat16)
        own = ((lane >= N_SPLIT * (g + 1)) & (lane < N_SPLIT * (g + 2))).astype(jnp.float32)
        ext_q = jnp.dot(cat, sel_q, preferred_element_type=jnp.float32) + own
        qa_o[:, h * AUG_W:h * AUG_W + HEAD_DIM] = (q_ref[:, h * HEAD_DIM:(h + 1) * HEAD_DIM] * FOX_SCALE).astype(qa_o.dtype)
        qa_o[:, h * AUG_W + HEAD_DIM:(h + 1) * AUG_W] = ext_q.astype(qa_o.dtype)


def _fox_aug(z, cum, tr):
    m = z.shape[0]
    kw = FOX_KV_HEADS * HEAD_DIM
    return pl.pallas_call(
        _fox_aug_kernel,
        grid=(m // tr,),
        in_specs=[pl.BlockSpec((tr, FOX_W), lambda i: (i, Z_QF // FOX_W)),
                  pl.BlockSpec((tr, kw), lambda i: (i, Z_KF // kw)),
                  pl.BlockSpec((tr, LANES), lambda i: (i, 0))],
        out_specs=[pl.BlockSpec((tr, FOX_HEADS * AUG_W), lambda i: (i, 0)),
                   pl.BlockSpec((tr, FOX_KV_HEADS * AUG_W), lambda i: (i, 0))],
        out_shape=[jax.ShapeDtypeStruct((m, FOX_HEADS * AUG_W), jnp.bfloat16),
                   jax.ShapeDtypeStruct((m, FOX_KV_HEADS * AUG_W), jnp.bfloat16)],
        compiler_params=_cparams("parallel"),
        name="fox_aug",
    )(z, z, cum)


ATT_BLK = 256


def _flash_kernel(*refs, n_q, k_shared, v_shared, hs):
    n_k = len(k_shared)
    q_refs = refs[:n_q]
    k_refs = refs[n_q:n_q + n_k]
    v_ref = refs[n_q + n_k]
    o_ref = refs[n_q + n_k + 1]
    k_scr, v_scr, q_scr, m_scr, l_scr, acc_scr = refs[n_q + n_k + 2:]
    t = o_ref.shape[0]
    _, t_pad, dk = k_scr.shape
    dv = v_scr.shape[2]
    n_full, tail = t // ATT_BLK, t % ATT_BLK
    per_head_k = k_scr.shape[0] > 1
    per_head_v = v_scr.shape[0] > 1

    for j in range(k_scr.shape[0]):
        k_scr[j, pl.ds(t, t_pad - t), :] = jnp.zeros((t_pad - t, dk), k_scr.dtype)
        off = 0
        for r, shared in zip(k_refs, k_shared):
            w = r.shape[1] if shared else r.shape[1] // hs
            src = r[...] if shared else r[:, j * w:(j + 1) * w]
            k_scr[j, 0:t, off:off + w] = src.astype(k_scr.dtype)
            off += w
    for j in range(v_scr.shape[0]):
        v_scr[j, pl.ds(t, t_pad - t), :] = jnp.zeros((t_pad - t, dv), v_scr.dtype)
        src = v_ref[...] if v_shared else v_ref[:, j * dv:(j + 1) * dv]
        v_scr[j, 0:t, :] = src.astype(v_scr.dtype)

    def q_block(rows, tq, n_off, n_masked):
        mrows = hs * tq
        for j in range(hs):
            off = 0
            for r in q_refs:
                w = r.shape[1] // hs
                q_scr[j * tq:(j + 1) * tq, off:off + w] = r[rows, j * w:(j + 1) * w]
                off += w
        m_scr[0:mrows, :] = jnp.full((mrows, LANES), NEG_INF, jnp.float32)
        l_scr[0:mrows, :] = jnp.zeros((mrows, LANES), jnp.float32)
        acc_scr[0:mrows, :] = jnp.zeros((mrows, dv), jnp.float32)

        def attend(kj, mask_shift):
            krows = pl.ds(pl.multiple_of(kj * ATT_BLK, ATT_BLK), ATT_BLK)
            for j in range(hs):
                sl = slice(j * tq, (j + 1) * tq)
                s = lax.dot_general(q_scr[sl, :], k_scr[j if per_head_k else 0, krows, :], NT_DIMS,
                                    preferred_element_type=jnp.float32)
                if mask_shift is not None:
                    r_in = lax.broadcasted_iota(jnp.int32, (tq, ATT_BLK), 0)
                    c_in = lax.broadcasted_iota(jnp.int32, (tq, ATT_BLK), 1) + mask_shift
                    s = jnp.where(c_in <= r_in, s, NEG_INF)
                s0, s1 = s[:, 0:LANES], s[:, LANES:]
                m_old = m_scr[sl, :]
                m_new = jnp.maximum(m_old, jnp.max(jnp.maximum(s0, s1), axis=-1, keepdims=True))
                alpha = jnp.exp(m_old - m_new)
                p0 = jnp.exp(s0 - m_new)
                p1 = jnp.exp(s1 - m_new)
                l_scr[sl, :] = alpha * l_scr[sl, :] + jnp.sum(p0 + p1, axis=-1, keepdims=True)
                m_scr[sl, :] = m_new
                p = jnp.concatenate([p0, p1], axis=1).astype(v_scr.dtype)
                acc_scr[sl, :] = alpha * acc_scr[sl, :] + jnp.dot(
                    p, v_scr[j if per_head_v else 0, krows, :], preferred_element_type=jnp.float32)

        def off_diag(kj, carry):
            attend(kj, None)
            return carry

        lax.fori_loop(0, n_off, off_diag, 0)
        for d in range(n_masked):
            attend(n_off + d, d * ATT_BLK)
        for j in range(hs):
            sl = slice(j * tq, (j + 1) * tq)
            o_ref[rows, j * dv:(j + 1) * dv] = (acc_scr[sl, :] / l_scr[sl, :]).astype(o_ref.dtype)

    def main(qi, carry):
        q_block(pl.ds(pl.multiple_of(qi * ATT_BLK, ATT_BLK), ATT_BLK), ATT_BLK, qi, 1)
        return carry

    lax.fori_loop(0, n_full - 1, main, 0)
    q_block(slice((n_full - 1) * ATT_BLK, t), ATT_BLK + tail, n_full - 1, 2)


def _flash(q_parts, k_parts, v_part, batch, t, n_heads, hs, name):
    assert t % ATT_BLK and (t % ATT_BLK) % 16 == 0 and n_heads % hs == 0
    t_pad = _round_up(t, ATT_BLK)
    dv = V_DIM
    in_specs, operands = [], []
    for arr in q_parts:
        w = arr.shape[1] // n_heads
        in_specs.append(pl.BlockSpec((t, hs * w), lambda b, g: (b, g)))
        operands.append(arr)

    def kv_spec(part):
        _, w, col0, share = part
        if share:
            assert col0 % w == 0 and share % hs == 0
            return pl.BlockSpec((t, w), lambda b, g: (b, col0 // w + (g * hs) // share)), True
        assert col0 % (hs * w) == 0
        return pl.BlockSpec((t, hs * w), lambda b, g: (b, col0 // (hs * w) + g)), False

    shared = []
    for part in list(k_parts) + [v_part]:
        spec, is_shared = kv_spec(part)
        in_specs.append(spec)
        operands.append(part[0])
        shared.append(is_shared)
    v_shared = shared.pop()
    dk = sum(p[1] for p in k_parts)
    mrows = hs * (ATT_BLK + t % ATT_BLK)
    scratch = [pltpu.VMEM((1 if all(shared) else hs, t_pad, dk), jnp.bfloat16),
               pltpu.VMEM((1 if v_shared else hs, t_pad, dv), jnp.bfloat16),
               pltpu.VMEM((mrows, dk), jnp.bfloat16), pltpu.VMEM((mrows, LANES), jnp.float32),
               pltpu.VMEM((mrows, LANES), jnp.float32), pltpu.VMEM((mrows, dv), jnp.float32)]
    return pl.pallas_call(
        functools.partial(_flash_kernel, n_q=len(q_parts), k_shared=tuple(shared), v_shared=v_shared, hs=hs),
        grid=(batch, n_heads // hs),
        in_specs=in_specs,
        out_specs=pl.BlockSpec((t, hs * dv), lambda b, g: (b, g)),
        out_shape=jax.ShapeDtypeStruct((batch * t, n_heads * dv), jnp.float32),
        scratch_shapes=scratch,
        compiler_params=_cparams("parallel", "parallel"),
        name=name,
    )(*operands)


def _head_mm_kernel(a_ref, b_ref, o_ref, *, scale):
    acc = jnp.dot(a_ref[...], b_ref[0], preferred_element_type=jnp.float32)
    if scale != 1.0:
        acc = acc * scale
    o_ref[...] = acc.astype(o_ref.dtype)


def _head_mm(a, w, out_dtype, tm, scale, name):
    m = a.shape[0]
    n_heads, kdim, n = w.shape
    return pl.pallas_call(
        functools.partial(_head_mm_kernel, scale=scale),
        grid=(n_heads, m // tm),
        in_specs=[pl.BlockSpec((tm, kdim), lambda h, i: (i, h)),
                  pl.BlockSpec((1, kdim, n), lambda h, i: (h, 0, 0))],
        out_specs=pl.BlockSpec((tm, n), lambda h, i: (i, h)),
        out_shape=jax.ShapeDtypeStruct((m, n_heads * n), out_dtype),
        compiler_params=_cparams("parallel", "parallel"),
        name=name,
    )(a, w)


PAGES_PER_STEP = 32
Q_ROWS = 64
N_CACHES = 5


def _page_copies(pt_ref, caches, bufs, sems, seq, step, slot, n_steps):
    kc, vc, cc, rc, lc = caches
    kbuf, vbuf, cbuf, rbuf, lbuf = bufs
    npg = PAGES_PER_STEP
    copies = []
    for p in range(npg):
        page = pt_ref[seq, (n_steps - 1 - step) * npg + p]
        rows2 = pl.ds(p * PAGE_SIZE * FOX_KV_HEADS, PAGE_SIZE * FOX_KV_HEADS)
        copies.append(pltpu.make_async_copy(kc.at[page], kbuf.at[slot, rows2, :], sems.at[slot, 0]))
        copies.append(pltpu.make_async_copy(vc.at[page], vbuf.at[slot, rows2, :], sems.at[slot, 1]))
        copies.append(pltpu.make_async_copy(cc.at[page], cbuf.at[slot, pl.ds(p * PAGE_SIZE, PAGE_SIZE), :],
                                            sems.at[slot, 2]))
        copies.append(pltpu.make_async_copy(rc.at[page], rbuf.at[slot, :, pl.ds(p * PAGE_SIZE, PAGE_SIZE)],
                                            sems.at[slot, 3]))
        copies.append(pltpu.make_async_copy(lc.at[page], lbuf.at[slot, pl.ds(p * FOX_HEADS, FOX_HEADS), :],
                                            sems.at[slot, 4]))
    return copies


def _decode_kernel(pt_ref, kc, vc, cc, rc, lc, kn_ref, vn_ref, cn_ref, rn_ref, ln_ref, qf_ref, qa_ref, qr_ref,
                   of_ref, ol_ref,
                   kbuf, vbuf, cbuf, rbuf, lbuf, sems, mf_scr, lf_scr, af_scr, mm_scr, lm_scr, am_scr, run_scr,
                   qt_scr, *, n_steps):
    caches = (kc, vc, cc, rc, lc)
    bufs = (kbuf, vbuf, cbuf, rbuf, lbuf)
    seq = pl.program_id(0)
    step = pl.program_id(1)
    n_seq = pl.num_programs(0)
    slot = step % 2
    half = Q_ROWS // FOX_KV_HEADS
    n_tok = half // FOX_GROUP

    @pl.when(jnp.logical_and(seq == 0, step == 0))
    def _():
        for cp in _page_copies(pt_ref, caches, bufs, sems, seq, step, slot, n_steps):
            cp.start()

    last = jnp.logical_and(seq == n_seq - 1, step == n_steps - 1)

    @pl.when(jnp.logical_not(last))
    def _():
        wrap = step == n_steps - 1
        nseq = jnp.where(wrap, seq + 1, seq)
        nstep = jnp.where(wrap, 0, step + 1)
        for cp in _page_copies(pt_ref, caches, bufs, sems, nseq, nstep, 1 - slot, n_steps):
            cp.start()

    r = lax.broadcasted_iota(jnp.int32, (LANES, 2 * LANES), 0)
    c = lax.broadcasted_iota(jnp.int32, (LANES, 2 * LANES), 1)
    sfx_tot = ((r > c) | (c >= LANES)).astype(jnp.bfloat16)

    def attend(n_pages, lf_all, k_of, v_of, ckv_all, kr_all, is_new):
        width = n_pages * LANES
        st = _dot_f32_by01(lf_all, sfx_tot)
        run = run_scr[...]
        g_pages = [None] * n_pages
        for p in reversed(range(n_pages)):
            rows = slice(p * FOX_HEADS, (p + 1) * FOX_HEADS)
            g_pages[p] = run + st[rows, 0:LANES]
            run = run + st[rows, LANES:]
        run_scr[...] = run
        if is_new:
            for kvh in range(FOX_KV_HEADS):
                for i in range(n_tok):
                    col = g_pages[0][kvh * FOX_GROUP:(kvh + 1) * FOX_GROUP, i:i + 1]
                    row0 = kvh * half + i * FOX_GROUP
                    qt_scr[row0:row0 + FOX_GROUP, :] = jnp.broadcast_to(-col, (FOX_GROUP, LANES))
        slot_idx = lax.broadcasted_iota(jnp.int32, (1, width), 1)

        for kvh in range(FOX_KV_HEADS):
            rows = slice(kvh * half, (kvh + 1) * half)
            k = k_of(kvh).astype(jnp.bfloat16)
            v = v_of(kvh).astype(jnp.bfloat16)
            s = lax.dot_general(qf_ref[0, rows, :], k, NT_DIMS, preferred_element_type=jnp.float32)
            hrows = slice(kvh * FOX_GROUP, (kvh + 1) * FOX_GROUP)
            g = g_pages[0][hrows, :] if n_pages == 1 else jnp.concatenate([gp[hrows, :] for gp in g_pages], axis=1)
            s = s + jnp.concatenate([g] * n_tok, axis=0) + qt_scr[rows, 0:1]
            if is_new:
                tok = lax.broadcasted_iota(jnp.int32, (half, width), 0) // FOX_GROUP
                s = jnp.where(slot_idx <= tok, s, NEG_INF)
            m_old = mf_scr[rows, :]
            m_new = jnp.maximum(m_old, jnp.max(s, axis=-1, keepdims=True))
            alpha = jnp.exp(m_old - m_new)
            pr = jnp.exp(s - m_new[:, 0:1])
            lf_scr[rows, :] = alpha * lf_scr[rows, :] + jnp.sum(pr, axis=-1, keepdims=True)
            mf_scr[rows, :] = m_new
            af_scr[rows, :] = alpha * af_scr[rows, :] + jnp.dot(pr.astype(jnp.bfloat16), v,
                                                               preferred_element_type=jnp.float32)

        ckv = ckv_all.astype(jnp.bfloat16)
        s = (lax.dot_general(qa_ref[0], ckv, NT_DIMS, preferred_element_type=jnp.float32)
             + jnp.dot(qr_ref[0], kr_all.astype(jnp.bfloat16), preferred_element_type=jnp.float32))
        if is_new:
            tok = lax.broadcasted_iota(jnp.int32, (Q_ROWS, width), 0) % n_tok
            s = jnp.where(slot_idx <= tok, s, NEG_INF)
        m_old = mm_scr[...]
        m_new = jnp.maximum(m_old, jnp.max(s, axis=-1, keepdims=True))
        alpha = jnp.exp(m_old - m_new)
        pr = jnp.exp(s - m_new[:, 0:1])
        lm_scr[...] = alpha * lm_scr[...] + jnp.sum(pr, axis=-1, keepdims=True)
        mm_scr[...] = m_new
        am_scr[...] = alpha[:, 0:1] * am_scr[...] + jnp.dot(pr.astype(jnp.bfloat16), ckv,
                                                           preferred_element_type=jnp.float32)

    @pl.when(step == 0)
    def _():
        mf_scr[...] = jnp.full_like(mf_scr, NEG_INF)
        lf_scr[...] = jnp.zeros_like(lf_scr)
        af_scr[...] = jnp.zeros_like(af_scr)
        mm_scr[...] = jnp.full_like(mm_scr, NEG_INF)
        lm_scr[...] = jnp.zeros_like(lm_scr)
        am_scr[...] = jnp.zeros_like(am_scr)
        run_scr[...] = jnp.zeros_like(run_scr)
        strided = lambda ref: (lambda kvh: ref[0, pl.ds(kvh, PAGE_SIZE, stride=FOX_KV_HEADS), :])
        attend(1, ln_ref[0], strided(kn_ref), strided(vn_ref), cn_ref[0], rn_ref[0], True)

    for cp in _page_copies(pt_ref, caches, bufs, sems, seq, step, slot, n_steps):
        cp.wait()

    npg = PAGES_PER_STEP
    strided_buf = lambda buf: (lambda kvh: buf[slot, pl.ds(kvh, npg * PAGE_SIZE, stride=FOX_KV_HEADS), :])
    attend(npg, lbuf[slot], strided_buf(kbuf), strided_buf(vbuf), cbuf[slot], rbuf[slot], False)

    @pl.when(step == n_steps - 1)
    def _():
        of_ref[0] = af_scr[...] / lf_scr[...]
        ol_ref[0] = (am_scr[...] / lm_scr[:, 0:1]).astype(ol_ref.dtype)


def _decode(page_table, caches, news, qf, qa, qr):
    n_seq, n_pages = page_table.shape
    npg = PAGES_PER_STEP
    n_steps = n_pages // npg
    assert n_pages % npg == 0 and n_steps % 2 == 0
    per_seq = lambda arr: pl.BlockSpec((1,) + tuple(arr.shape[1:]), lambda b, s, pt: (b, 0, 0))
    any_spec = pl.BlockSpec(memory_space=pl.ANY)
    in_specs = [any_spec] * N_CACHES + [per_seq(a) for a in news] + [per_seq(a) for a in (qf, qa, qr)]
    rows2 = npg * PAGE_SIZE * FOX_KV_HEADS
    grid_spec = pltpu.PrefetchScalarGridSpec(
        num_scalar_prefetch=1,
        grid=(n_seq, n_steps),
        in_specs=in_specs,
        out_specs=[pl.BlockSpec((1, Q_ROWS, HEAD_DIM), lambda b, s, pt: (b, 0, 0)),
                   pl.BlockSpec((1, Q_ROWS, KV_LORA), lambda b, s, pt: (b, 0, 0))],
        scratch_shapes=[pltpu.VMEM((2, rows2, HEAD_DIM), jnp.float32), pltpu.VMEM((2, rows2, HEAD_DIM), jnp.float32),
                        pltpu.VMEM((2, npg * PAGE_SIZE, KV_LORA), jnp.float32),
                        pltpu.VMEM((2, ROPE_DIM, npg * PAGE_SIZE), jnp.float32),
                        pltpu.VMEM((2, npg * FOX_HEADS, LANES), jnp.float32),
                        pltpu.SemaphoreType.DMA((2, N_CACHES)),
                        pltpu.VMEM((Q_ROWS, LANES), jnp.float32), pltpu.VMEM((Q_ROWS, LANES), jnp.float32),
                        pltpu.VMEM((Q_ROWS, HEAD_DIM), jnp.float32),
                        pltpu.VMEM((Q_ROWS, LANES), jnp.float32), pltpu.VMEM((Q_ROWS, LANES), jnp.float32),
                        pltpu.VMEM((Q_ROWS, KV_LORA), jnp.float32),
                        pltpu.VMEM((FOX_HEADS, LANES), jnp.float32), pltpu.VMEM((Q_ROWS, LANES), jnp.float32)],
    )
    return pl.pallas_call(
        functools.partial(_decode_kernel, n_steps=n_steps),
        grid_spec=grid_spec,
        out_shape=[jax.ShapeDtypeStruct((n_seq, Q_ROWS, HEAD_DIM), jnp.float32),
                   jax.ShapeDtypeStruct((n_seq, Q_ROWS, KV_LORA), jnp.bfloat16)],
        compiler_params=_cparams("arbitrary", "arbitrary"),
        name="decode_attention",
    )(page_table, *caches, *news, qf, qa, qr)


def _rope_tables(pos):
    half = ROPE_DIM // 2
    inv = ROPE_BASE ** (-jnp.arange(half, dtype=jnp.float32) / half)
    ang = pos.astype(jnp.float32)[:, None] * inv[None, :]
    cos, sin = jnp.cos(ang), jnp.sin(ang)
    z = jnp.zeros_like(cos)
    pad = jnp.zeros((pos.shape[0], LANES - ROPE_DIM), jnp.float32)
    cos_t = jnp.concatenate([cos, cos, pad], axis=1)
    sa_t = jnp.concatenate([-sin, z, pad], axis=1)
    sb_t = jnp.concatenate([z, sin, pad], axis=1)
    return cos_t, sa_t, sb_t


def _prep_weights(w_in, b_f, w_q_up, w_uk, w_uv, w_o, w_up, w_down):
    bf = jnp.bfloat16
    offs = [0]
    for s in (FOX_W, FOX_KV_HEADS * HEAD_DIM, FOX_KV_HEADS * HEAD_DIM, FOX_HEADS, Q_LORA, KV_LORA, ROPE_DIM):
        offs.append(offs[-1] + s)
    q_f, k_f, v_f, f_l, q_lat, c_kv, k_r = [w_in[:, offs[i]:offs[i + 1]] for i in range(7)]
    zc = lambda n: jnp.zeros((D_MODEL, n), w_in.dtype)
    w_in_p = jnp.concatenate([q_f, q_lat, c_kv, k_f, v_f, k_r, zc(LANES - ROPE_DIM), f_l, zc(LANES - FOX_HEADS),
                              zc(Z_W - Z_USED)], axis=1).astype(bf)
    b_f_pad = jnp.concatenate([b_f, jnp.zeros((LANES - FOX_HEADS,), b_f.dtype)]).reshape(1, LANES)
    wq = w_q_up.reshape(Q_LORA, MLA_HEADS, NOPE_DIM + ROPE_DIM)
    wq_nope = wq[:, :, :NOPE_DIM].reshape(Q_LORA, MLA_HEADS * NOPE_DIM)
    wq_rope = jnp.concatenate([wq[:, :, NOPE_DIM:], jnp.zeros((Q_LORA, MLA_HEADS, LANES - ROPE_DIM), wq.dtype)],
                              axis=2).reshape(Q_LORA, MLA_HEADS * LANES)
    wq_p = jnp.concatenate([wq_nope, wq_rope], axis=1).astype(bf)
    wk_all = w_uk.transpose(1, 0, 2).reshape(KV_LORA, MLA_HEADS * NOPE_DIM)
    wv_all = w_uv.transpose(1, 0, 2).reshape(KV_LORA, MLA_W)
    w_kv = jnp.concatenate([wk_all, wv_all], axis=1).astype(bf)
    w_uk_t = w_uk.transpose(0, 2, 1).astype(bf)
    return w_in_p, b_f_pad, wq_p, w_kv, w_uk_t, w_uv.astype(bf), w_o, w_up, w_down


def _dense_tail(x, o_f, o_m, g, wts, tm, tr, tag):
    w_o, w_up, w_down = wts
    o = _fin1(o_f, o_m, g["fox_out"], g["mla_out"], tr, "fin1_" + tag)
    y = _matmul(o, w_o, jnp.float32, tm, 512, name="mm_o_" + tag)
    x1, h2 = _fin2(x, y, g["post_mix"], g["pre_mlp"], tr, "fin2_" + tag)
    u = _matmul(h2, w_up, jnp.bfloat16, tm, 512, relu2=True, name="mm_up_" + tag)
    d = _matmul(u, w_down, jnp.float32, tm, 1024, tk=2048, name="mm_down_" + tag)
    return _fin3(x1, d, g["post_mlp"], tr, "fin3_" + tag)


def kernel(x_prompt, x_sample, cache_fox_k, cache_fox_v, cache_fox_logf, cache_mla_ckv, cache_mla_krope,
           page_table, meta_tokens, g_pre_mix, w_in, b_f, g_q_lat, w_q_up, w_uk, g_kv_lat, w_uv, g_fox_out,
           g_mla_out, w_o, g_post_mix, g_pre_mlp, w_up, w_down, g_post_mlp):
    depth = w_in.shape[0]
    assert depth == 1
    batch, seq, _ = x_prompt.shape
    t = seq + N_META
    n_seq, dec_seq, _ = x_sample.shape
    n_past = page_table.shape[1] * PAGE_SIZE
    assert dec_seq * FOX_HEADS == Q_ROWS

    (w_in_p, b_f_pad, wq_p, w_kv, w_uk_t, w_uv_b, w_o_b, w_up_b, w_down_b) = _prep_weights(
        w_in[0], b_f[0], w_q_up[0], w_uk[0], w_uv[0], w_o[0], w_up[0], w_down[0])
    g = {"fox_out": g_fox_out[0], "mla_out": g_mla_out[0], "post_mix": g_post_mix[0],
         "pre_mlp": g_pre_mlp[0], "post_mlp": g_post_mlp[0]}
    tail_w = (w_o_b, w_up_b, w_down_b)

    meta = jnp.broadcast_to(meta_tokens[None].astype(x_prompt.dtype), (batch, N_META, D_MODEL))
    xp = jnp.concatenate([meta, x_prompt], axis=1).reshape(batch * t, D_MODEL)
    tm_p, tr_wide, tr_p = 1376, 192, 688
    tabs_p = _rope_tables(jnp.arange(t))
    h = _rms_cast(xp, g_pre_mix[0], tr_wide, "rms_in_p")
    z = _matmul(h, w_in_p, jnp.float32, tm_p, 512, name="mm_in_p")
    qlat, ckv, ckvb, kr, krb, logf = _post_in(z, tabs_p, t // tr_p, g_q_lat[0], g_kv_lat[0], b_f_pad, tr_p,
                                              "post_in_p")
    q = _matmul(qlat, wq_p, jnp.float32, tm_p, 1024, name="mm_qup_p")
    qn, qr = _post_q(q, tabs_p, t // tr_p, tr_p, MLA_SCALE, "post_q_p")
    kv_up = _matmul(ckvb, w_kv, jnp.bfloat16, tm_p, 1024, name="mm_kvup_p")
    o_f, o_m = _prompt_attention(z, logf, qn, qr, kv_up, krb, batch, t, tr_p)
    yp = _dense_tail(xp, o_f, o_m, g, tail_w, tm_p, tr_wide, "p")

    k_f_p = z[:, Z_KF:Z_KF + FOX_KV_HEADS * HEAD_DIM].reshape(1, batch, t, FOX_KV_HEADS, HEAD_DIM)
    v_f_p = z[:, Z_VF:Z_VF + FOX_KV_HEADS * HEAD_DIM].reshape(1, batch, t, FOX_KV_HEADS, HEAD_DIM)
    logf_p = logf[:, :FOX_HEADS].reshape(1, batch, t, FOX_HEADS)
    ckv_p = ckv.reshape(1, batch, t, KV_LORA)
    kr_p = kr[:, :ROPE_DIM].reshape(1, batch, t, ROPE_DIM)
    y_prompt = yp.reshape(batch, t, D_MODEL)[:, N_META:]

    ms = n_seq * dec_seq
    xs = x_sample.reshape(ms, D_MODEL)
    tabs_s = _rope_tables(n_past + jnp.arange(dec_seq))
    tabs_s = tuple(jnp.tile(tb, (ms // dec_seq, 1)) for tb in tabs_s)
    hs = _rms_cast(xs, g_pre_mix[0], 128, "rms_in_s")
    zs = _matmul(hs, w_in_p, jnp.float32, ms, 512, name="mm_in_s")
    qlat_s, ckv_s, _, kr_s, _, logf_s = _post_in(zs, tabs_s, 1, g_q_lat[0], g_kv_lat[0], b_f_pad, ms, "post_in_s")
    q_s = _matmul(qlat_s, wq_p, jnp.float32, ms, 1024, name="mm_qup_s")
    qn_s, qr_s = _post_q(q_s, tabs_s, 1, ms, 1.0, "post_q_s")
    qa_s = _head_mm(qn_s, w_uk_t, jnp.bfloat16, ms, MLA_SCALE, "q_absorb_s")

    kf_s = zs[:, Z_KF:Z_KF + FOX_KV_HEADS * HEAD_DIM]
    vf_s = zs[:, Z_VF:Z_VF + FOX_KV_HEADS * HEAD_DIM]
    o_f_s, ol_s = _sample_attention(zs, qa_s, qr_s, ckv_s, kr_s, logf_s, cache_fox_k[0], cache_fox_v[0],
                                    cache_fox_logf[0], cache_mla_ckv[0], cache_mla_krope[0], page_table,
                                    n_seq, dec_seq)
    o_m_s = _head_mm(ol_s, w_uv_b, jnp.float32, ms, 1.0, "o_unabsorb_s")
    ys = _dense_tail(xs, o_f_s, o_m_s, g, tail_w, ms, 128, "s")

    k_f_s = kf_s.reshape(1, n_seq, dec_seq, FOX_KV_HEADS, HEAD_DIM)
    v_f_s = vf_s.reshape(1, n_seq, dec_seq, FOX_KV_HEADS, HEAD_DIM)
    logf_s_o = logf_s[:, :FOX_HEADS].reshape(1, n_seq, dec_seq, FOX_HEADS)
    ckv_s_o = ckv_s.reshape(1, n_seq, dec_seq, KV_LORA)
    kr_s_o = kr_s[:, :ROPE_DIM].reshape(1, n_seq, dec_seq, ROPE_DIM)
    y_sample = ys.reshape(n_seq, dec_seq, D_MODEL)

    return (y_prompt, y_sample, k_f_p, v_f_p, logf_p, ckv_p, kr_p, k_f_s, v_f_s, logf_s_o, ckv_s_o, kr_s_o)


def _prompt_attention(z, logf, qn, qr, kv_up, krb, batch, t, tr):
    cum = _cum_prompt(logf, batch, t)
    q_aug, k_aug = _fox_aug(z, cum, tr)
    o_f = _flash([q_aug], [(k_aug, AUG_W, 0, FOX_GROUP)], (z, HEAD_DIM, Z_VF, FOX_GROUP), batch, t,
                 FOX_HEADS, 8, "flash_prompt_fox")
    o_m = _flash([qn, qr], [(kv_up, NOPE_DIM, 0, 0), (krb, LANES, 0, MLA_HEADS)],
                 (kv_up, V_DIM, MLA_HEADS * NOPE_DIM, 0), batch, t, MLA_HEADS, 4, "flash_prompt_mla")
    return o_f, o_m


def _sample_attention(zs, qa_s, qr_s, ckv_s, kr_s, logf_s, cache_k, cache_v, cache_lf, cache_ckv, cache_kr,
                      page_table, n_seq, dec_seq):
    ms = n_seq * dec_seq
    kf_s = zs[:, Z_KF:Z_KF + FOX_KV_HEADS * HEAD_DIM]
    vf_s = zs[:, Z_VF:Z_VF + FOX_KV_HEADS * HEAD_DIM]
    qf_d = (zs[:, Z_QF:Z_QF + FOX_W] * FOX_SCALE).astype(jnp.bfloat16)
    qf_d = qf_d.reshape(n_seq, dec_seq, FOX_KV_HEADS, FOX_GROUP, HEAD_DIM)
    qf_d = qf_d.transpose(0, 2, 1, 3, 4).reshape(n_seq, Q_ROWS, HEAD_DIM)
    qa_d = qa_s.reshape(n_seq, dec_seq, MLA_HEADS, KV_LORA).transpose(0, 2, 1, 3).reshape(n_seq, Q_ROWS, KV_LORA)
    qr_d = qr_s.reshape(n_seq, dec_seq, MLA_HEADS, LANES)[..., :ROPE_DIM].transpose(0, 2, 1, 3)
    qr_d = qr_d.reshape(n_seq, Q_ROWS, ROPE_DIM)
    pad_slots = lambda a, rows: jnp.pad(a, ((0, 0), (0, rows - a.shape[1]), (0, 0)))
    pad_lanes = lambda a: jnp.pad(a, ((0, 0), (0, 0), (0, PAGE_SIZE - a.shape[2])))
    k_new = pad_slots(kf_s.reshape(n_seq, dec_seq * FOX_KV_HEADS, HEAD_DIM), PAGE_SIZE * FOX_KV_HEADS)
    v_new = pad_slots(vf_s.reshape(n_seq, dec_seq * FOX_KV_HEADS, HEAD_DIM), PAGE_SIZE * FOX_KV_HEADS)
    ckv_new = pad_slots(ckv_s.reshape(n_seq, dec_seq, KV_LORA), PAGE_SIZE)
    kr_new = pad_lanes(kr_s[:, :ROPE_DIM].reshape(n_seq, dec_seq, ROPE_DIM).transpose(0, 2, 1))
    lf_new = pad_lanes(logf_s[:, :FOX_HEADS].reshape(n_seq, dec_seq, FOX_HEADS).transpose(0, 2, 1))
    n_pool = cache_k.shape[0]
    caches = (cache_k.reshape(n_pool, PAGE_SIZE * FOX_KV_HEADS, HEAD_DIM),
              cache_v.reshape(n_pool, PAGE_SIZE * FOX_KV_HEADS, HEAD_DIM),
              cache_ckv, cache_kr.transpose(0, 2, 1), cache_lf.transpose(0, 2, 1))
    of_d, ol_d = _decode(page_table, caches, (k_new, v_new, ckv_new, kr_new, lf_new), qf_d, qa_d, qr_d)
    o_f_s = of_d.reshape(n_seq, FOX_KV_HEADS, dec_seq, FOX_GROUP, HEAD_DIM).transpose(0, 2, 1, 3, 4)
    o_f_s = o_f_s.reshape(ms, FOX_W)
    ol_s = ol_d.reshape(n_seq, MLA_HEADS, dec_seq, KV_LORA).transpose(0, 2, 1, 3).reshape(ms, MLA_HEADS * KV_LORA)
    return o_f_s, ol_s
```

```python
import functools

import jax
import jax.numpy as jnp
from jax import lax
from jax.experimental import pallas as pl
from jax.experimental.pallas import tpu as pltpu

D_MODEL = 4096
N_META = 16
HEAD_DIM = 128
FOX_HEADS = 16
FOX_KV_HEADS = 2
FOX_GROUP = FOX_HEADS // FOX_KV_HEADS
MLA_HEADS = 16
Q_LORA = 1024
KV_LORA = 512
NOPE_DIM = 128
ROPE_DIM = 64
V_DIM = 128
ROPE_BASE = 10000.0
D_FF = 4 * D_MODEL
EPS = 1e-6
FOX_W = FOX_HEADS * HEAD_DIM
MLA_W = MLA_HEADS * V_DIM
PAGE_SIZE = 128
FOX_SCALE = HEAD_DIM ** -0.5
MLA_SCALE = (NOPE_DIM + ROPE_DIM) ** -0.5

LANES = 128
VMEM_LIMIT_BYTES = 56 * 1024 * 1024

Z_QF = 0
Z_QLAT = FOX_W
Z_CKV = Z_QLAT + Q_LORA
Z_KF = Z_CKV + KV_LORA
Z_VF = Z_KF + FOX_KV_HEADS * HEAD_DIM
Z_KR = Z_VF + FOX_KV_HEADS * HEAD_DIM
Z_F = Z_KR + LANES
Z_USED = Z_F + LANES
Z_W = 4608

NEG_INF = float("-inf")
NT_DIMS = (((1,), (1,)), ((), ()))


def _cparams(*sem):
    return pltpu.CompilerParams(dimension_semantics=sem, vmem_limit_bytes=VMEM_LIMIT_BYTES)


def _mm_kernel(a_ref, b_ref, o_ref, *, relu2):
    acc = jnp.dot(a_ref[...], b_ref[...].astype(a_ref.dtype), preferred_element_type=jnp.float32)
    if relu2:
        acc = jnp.square(jnp.maximum(acc, 0.0))
    o_ref[...] = acc.astype(o_ref.dtype)


def _mm_acc_kernel(a_ref, b_ref, o_ref, acc_ref):
    k = pl.program_id(2)

    @pl.when(k == 0)
    def _():
        acc_ref[...] = jnp.zeros_like(acc_ref)

    acc_ref[...] += jnp.dot(a_ref[...], b_ref[...].astype(a_ref.dtype), preferred_element_type=jnp.float32)

    @pl.when(k == pl.num_programs(2) - 1)
    def _():
        o_ref[...] = acc_ref[...].astype(o_ref.dtype)


def _matmul(a, b, out_dtype, tm, tn, *, tk=None, relu2=False, name):
    m, kdim = a.shape
    _, n = b.shape
    assert m % tm == 0 and n % tn == 0
    if tk is None:
        return pl.pallas_call(
            functools.partial(_mm_kernel, relu2=relu2),
            grid=(m // tm, n // tn),
            in_specs=[pl.BlockSpec((tm, kdim), lambda i, j: (i, 0)),
                      pl.BlockSpec((kdim, tn), lambda i, j: (0, j))],
            out_specs=pl.BlockSpec((tm, tn), lambda i, j: (i, j)),
            out_shape=jax.ShapeDtypeStruct((m, n), out_dtype),
            compiler_params=_cparams("parallel", "parallel"),
            name=name,
        )(a, b)
    assert kdim % tk == 0 and not relu2
    return pl.pallas_call(
        _mm_acc_kernel,
        grid=(m // tm, n // tn, kdim // tk),
        in_specs=[pl.BlockSpec((tm, tk), lambda i, j, k: (i, k)),
                  pl.BlockSpec((tk, tn), lambda i, j, k: (k, j))],
        out_specs=pl.BlockSpec((tm, tn), lambda i, j, k: (i, j)),
        out_shape=jax.ShapeDtypeStruct((m, n), out_dtype),
        scratch_shapes=[pltpu.VMEM((tm, tn), jnp.float32)],
        compiler_params=_cparams("parallel", "parallel", "arbitrary"),
        name=name,
    )(a, b)


def _rms(x, g):
    return x * lax.rsqrt(jnp.mean(x * x, axis=-1, keepdims=True) + EPS) * g


def _rms_cast_kernel(x_ref, g_ref, o_ref):
    o_ref[...] = _rms(x_ref[...], g_ref[...]).astype(o_ref.dtype)


def _rms_cast(x, g, tr, name):
    m, d = x.shape
    return pl.pallas_call(
        _rms_cast_kernel,
        grid=(m // tr,),
        in_specs=[pl.BlockSpec((tr, d), lambda i: (i, 0)),
                  pl.BlockSpec((1, d), lambda i: (0, 0))],
        out_specs=pl.BlockSpec((tr, d), lambda i: (i, 0)),
        out_shape=jax.ShapeDtypeStruct((m, d), jnp.bfloat16),
        compiler_params=_cparams("parallel"),
        name=name,
    )(x, g.reshape(1, d))


def _rope128(x, cos_t, sa_t, sb_t):
    n = x.shape[-1]
    half = ROPE_DIM // 2
    return x * cos_t + pltpu.roll(x, n - half, 1) * sa_t + pltpu.roll(x, half, 1) * sb_t


def _post_in_kernel(qlat_ref, ckv_ref, kr_ref, f_ref, cos_ref, sa_ref, sb_ref,
                    gq_ref, gkv_ref, bf_ref,
                    qlat_o, ckv_o, ckvb_o, kr_o, krb_o, logf_o):
    qlat_o[...] = _rms(qlat_ref[...], gq_ref[...]).astype(qlat_o.dtype)
    ckv = _rms(ckv_ref[...], gkv_ref[...])
    ckv_o[...] = ckv
    ckvb_o[...] = ckv.astype(ckvb_o.dtype)
    kr = _rope128(kr_ref[...], cos_ref[...], sa_ref[...], sb_ref[...])
    kr_o[...] = kr
    krb_o[...] = kr.astype(krb_o.dtype)
    x = f_ref[...] + bf_ref[...]
    logf_o[...] = jnp.minimum(x, 0.0) - jnp.log1p(jnp.exp(-jnp.abs(x)))


def _post_in(z, tabs, tab_blocks, g_q_lat, g_kv_lat, b_f_pad, tr, name):
    m = z.shape[0]
    cos_t, sa_t, sb_t = tabs
    tab_spec = pl.BlockSpec((tr, LANES), lambda i: (i % tab_blocks, 0))
    row = lambda w, off: pl.BlockSpec((tr, w), lambda i: (i, off // w))
    const = lambda w: pl.BlockSpec((1, w), lambda i: (0, 0))
    return pl.pallas_call(
        _post_in_kernel,
        grid=(m // tr,),
        in_specs=[row(Q_LORA, Z_QLAT), row(KV_LORA, Z_CKV), row(LANES, Z_KR), row(LANES, Z_F),
                  tab_spec, tab_spec, tab_spec, const(Q_LORA), const(KV_LORA), const(LANES)],
        out_specs=[pl.BlockSpec((tr, Q_LORA), lambda i: (i, 0)),
                   pl.BlockSpec((tr, KV_LORA), lambda i: (i, 0)),
                   pl.BlockSpec((tr, KV_LORA), lambda i: (i, 0)),
                   pl.BlockSpec((tr, LANES), lambda i: (i, 0)),
                   pl.BlockSpec((tr, LANES), lambda i: (i, 0)),
                   pl.BlockSpec((tr, LANES), lambda i: (i, 0))],
        out_shape=[jax.ShapeDtypeStruct((m, Q_LORA), jnp.bfloat16),
                   jax.ShapeDtypeStruct((m, KV_LORA), jnp.float32),
                   jax.ShapeDtypeStruct((m, KV_LORA), jnp.bfloat16),
                   jax.ShapeDtypeStruct((m, LANES), jnp.float32),
                   jax.ShapeDtypeStruct((m, LANES), jnp.bfloat16),
                   jax.ShapeDtypeStruct((m, LANES), jnp.float32)],
        compiler_params=_cparams("parallel"),
        name=name,
    )(z, z, z, z, cos_t, sa_t, sb_t, g_q_lat.reshape(1, -1), g_kv_lat.reshape(1, -1), b_f_pad)


def _q_up_kernel(a_ref, b_ref, cos_ref, sa_ref, sb_ref, o_ref, *, nope_scale, nope_tiles):
    acc = jnp.dot(a_ref[...], b_ref[...], preferred_element_type=jnp.float32)
    j = pl.program_id(1)

    @pl.when(j < nope_tiles)
    def _():
        o_ref[...] = (acc if nope_scale == 1.0 else acc * nope_scale).astype(o_ref.dtype)

    @pl.when(j >= nope_tiles)
    def _():
        cos_t, sa_t, sb_t = cos_ref[...], sa_ref[...], sb_ref[...]
        for h in range(o_ref.shape[1] // LANES):
            sl = slice(h * LANES, (h + 1) * LANES)
            o_ref[:, sl] = (_rope128(acc[:, sl], cos_t, sa_t, sb_t) * MLA_SCALE).astype(o_ref.dtype)


def _q_up(qlat, wq, tabs, tab_blocks, tm, tn, nope_scale, name):
    m, kdim = qlat.shape
    n = wq.shape[1]
    nope_w = MLA_HEADS * NOPE_DIM
    assert m % tm == 0 and n % tn == 0 and nope_w % tn == 0
    tab_spec = pl.BlockSpec((tm, LANES), lambda i, j: (i % tab_blocks, 0))
    return pl.pallas_call(
        functools.partial(_q_up_kernel, nope_scale=nope_scale, nope_tiles=nope_w // tn),
        grid=(m // tm, n // tn),
        in_specs=[pl.BlockSpec((tm, kdim), lambda i, j: (i, 0)), pl.BlockSpec((kdim, tn), lambda i, j: (0, j)),
                  tab_spec, tab_spec, tab_spec],
        out_specs=pl.BlockSpec((tm, tn), lambda i, j: (i, j)),
        out_shape=jax.ShapeDtypeStruct((m, n), jnp.bfloat16),
        compiler_params=_cparams("parallel", "parallel"),
        name=name,
    )(qlat, wq, *tabs)


def _fin1_kernel(of_ref, om_ref, gf_ref, gm_ref, o_ref):
    o_ref[:, :FOX_W] = _rms(of_ref[...], gf_ref[...]).astype(o_ref.dtype)
    o_ref[:, FOX_W:] = _rms(om_ref[...], gm_ref[...]).astype(o_ref.dtype)


def _fin1(o_f, o_m, g_f, g_m, tr, name):
    m = o_f.shape[0]
    return pl.pallas_call(
        _fin1_kernel,
        grid=(m // tr,),
        in_specs=[pl.BlockSpec((tr, FOX_W), lambda i: (i, 0)), pl.BlockSpec((tr, MLA_W), lambda i: (i, 0)),
                  pl.BlockSpec((1, FOX_W), lambda i: (0, 0)), pl.BlockSpec((1, MLA_W), lambda i: (0, 0))],
        out_specs=pl.BlockSpec((tr, FOX_W + MLA_W), lambda i: (i, 0)),
        out_shape=jax.ShapeDtypeStruct((m, FOX_W + MLA_W), jnp.bfloat16),
        compiler_params=_cparams("parallel"),
        name=name,
    )(o_f, o_m, g_f.reshape(1, -1), g_m.reshape(1, -1))


def _fin2_kernel(x_ref, y_ref, g1_ref, g2_ref, x1_o, h_o):
    x1 = x_ref[...] + _rms(y_ref[...], g1_ref[...])
    x1_o[...] = x1
    h_o[...] = _rms(x1, g2_ref[...]).astype(h_o.dtype)


def _fin2(x, y, g1, g2, tr, name):
    m, d = x.shape
    rs = pl.BlockSpec((tr, d), lambda i: (i, 0))
    gs = pl.BlockSpec((1, d), lambda i: (0, 0))
    return pl.pallas_call(
        _fin2_kernel,
        grid=(m // tr,),
        in_specs=[rs, rs, gs, gs],
        out_specs=[rs, rs],
        out_shape=[jax.ShapeDtypeStruct((m, d), jnp.float32), jax.ShapeDtypeStruct((m, d), jnp.bfloat16)],
        compiler_params=_cparams("parallel"),
        name=name,
    )(x, y, g1.reshape(1, d), g2.reshape(1, d))


def _fin3_kernel(x_ref, y_ref, g_ref, o_ref):
    o_ref[...] = x_ref[...] + _rms(y_ref[...], g_ref[...])


def _fin3(x, y, g, tr, name):
    m, d = x.shape
    rs = pl.BlockSpec((tr, d), lambda i: (i, 0))
    return pl.pallas_call(
        _fin3_kernel,
        grid=(m // tr,),
        in_specs=[rs, rs, pl.BlockSpec((1, d), lambda i: (0, 0))],
        out_specs=rs,
        out_shape=jax.ShapeDtypeStruct((m, d), jnp.float32),
        compiler_params=_cparams("parallel"),
        name=name,
    )(x, y, g.reshape(1, d))


def _split3(x):
    hi = x.astype(jnp.bfloat16)
    r1 = x - hi.astype(jnp.float32)
    mid = r1.astype(jnp.bfloat16)
    lo = (r1 - mid.astype(jnp.float32)).astype(jnp.bfloat16)
    return hi, mid, lo


def _dot_f32_by01(x, b01):
    hi, mid, lo = _split3(x)
    d = lambda a: jnp.dot(a, b01, preferred_element_type=jnp.float32)
    return d(hi) + d(mid) + d(lo)


def _dot01_by_f32(a01, x):
    hi, mid, lo = _split3(x)
    d = lambda b: jnp.dot(a01, b, preferred_element_type=jnp.float32)
    return d(hi) + d(mid) + d(lo)


def _round_up(n, m):
    return -(-n // m) * m


def _cum_prompt_kernel(lf_ref, c_ref, x_scr, c_scr):
    t = lf_ref.shape[0]
    x_scr[...] = jnp.zeros_like(x_scr)
    x_scr[0:t, :] = lf_ref[...]
    r = lax.broadcasted_iota(jnp.int32, (LANES, LANES), 0)
    c = lax.broadcasted_iota(jnp.int32, (LANES, LANES), 1)
    tri = (c <= r).astype(jnp.bfloat16)
    ones = jnp.ones((LANES, LANES), jnp.bfloat16)

    def body(k, carry):
        rows = pl.ds(pl.multiple_of(k * LANES, LANES), LANES)
        x = x_scr[rows, :]
        c_scr[rows, :] = _dot01_by_f32(tri, x) + carry
        return carry + _dot01_by_f32(ones, x)

    lax.fori_loop(0, x_scr.shape[0] // LANES, body, jnp.zeros((LANES, LANES), jnp.float32))
    c_ref[...] = c_scr[0:t, :]


def _cum_prompt(logf128, batch, t):
    tp = _round_up(t, LANES)
    return pl.pallas_call(
        _cum_prompt_kernel,
        grid=(batch,),
        in_specs=[pl.BlockSpec((t, LANES), lambda b: (b, 0))],
        out_specs=pl.BlockSpec((t, LANES), lambda b: (b, 0)),
        out_shape=jax.ShapeDtypeStruct((batch * t, LANES), jnp.float32),
        scratch_shapes=[pltpu.VMEM((tp, LANES), jnp.float32), pltpu.VMEM((tp, LANES), jnp.float32)],
        compiler_params=_cparams("parallel"),
        name="cum_prompt",
    )(logf128)


AUG_W = 2 * HEAD_DIM
N_SPLIT = 3


def _fox_aug_kernel(q_ref, k_ref, c_ref, qa_o, ka_o):
    hi, mid, lo = _split3(c_ref[...])
    cat = jnp.concatenate([hi, mid, lo], axis=1)
    ncat = -cat
    r = lax.broadcasted_iota(jnp.int32, (N_SPLIT * LANES, LANES), 0)
    col = lax.broadcasted_iota(jnp.int32, (N_SPLIT * LANES, LANES), 1)
    piece, head = r // LANES, r % LANES
    lane = lax.broadcasted_iota(jnp.int32, (1, LANES), 1)
    for kvh in range(FOX_KV_HEADS):
        g_of = head - kvh * FOX_GROUP
        sel_k = ((g_of >= 0) & (g_of < FOX_GROUP) & (col == N_SPLIT + N_SPLIT * g_of + piece)).astype(jnp.bfloat16)
        ext_k = jnp.dot(ncat, sel_k, preferred_element_type=jnp.float32) + (lane < N_SPLIT).astype(jnp.float32)
        ka_o[:, kvh * AUG_W:kvh * AUG_W + HEAD_DIM] = k_ref[:, kvh * HEAD_DIM:(kvh + 1) * HEAD_DIM].astype(ka_o.dtype)
        ka_o[:, kvh * AUG_W + HEAD_DIM:(kvh + 1) * AUG_W] = ext_k.astype(ka_o.dtype)
    for h in range(FOX_HEADS):
        g = h % FOX_GROUP
        sel_q = ((head == h) & (col == piece)).astype(jnp.bfloat16)
        own = ((lane >= N_SPLIT * (g + 1)) & (lane < N_SPLIT * (g + 2))).astype(jnp.float32)
        ext_q = jnp.dot(cat, sel_q, preferred_element_type=jnp.float32) + own
        qa_o[:, h * AUG_W:h * AUG_W + HEAD_DIM] = (q_ref[:, h * HEAD_DIM:(h + 1) * HEAD_DIM] * FOX_SCALE).astype(qa_o.dtype)
        qa_o[:, h * AUG_W + HEAD_DIM:(h + 1) * AUG_W] = ext_q.astype(qa_o.dtype)


def _fox_aug(z, cum, tr):
    m = z.shape[0]
    kw = FOX_KV_HEADS * HEAD_DIM
    return pl.pallas_call(
        _fox_aug_kernel,
        grid=(m // tr,),
        in_specs=[pl.BlockSpec((tr, FOX_W), lambda i: (i, Z_QF // FOX_W)),
                  pl.BlockSpec((tr, kw), lambda i: (i, Z_KF // kw)),
                  pl.BlockSpec((tr, LANES), lambda i: (i, 0))],
        out_specs=[pl.BlockSpec((tr, FOX_HEADS * AUG_W), lambda i: (i, 0)),
                   pl.BlockSpec((tr, FOX_KV_HEADS * AUG_W), lambda i: (i, 0))],
        out_shape=[jax.ShapeDtypeStruct((m, FOX_HEADS * AUG_W), jnp.bfloat16),
                   jax.ShapeDtypeStruct((m, FOX_KV_HEADS * AUG_W), jnp.bfloat16)],
        compiler_params=_cparams("parallel"),
        name="fox_aug",
    )(z, z, cum)


ATT_BLK = 256


def _flash_kernel(*refs, n_q, k_shared, v_shared, hs):
    n_k = len(k_shared)
    q_refs = refs[:n_q]
    k_refs = refs[n_q:n_q + n_k]
    v_ref = refs[n_q + n_k]
    o_ref = refs[n_q + n_k + 1]
    k_scr, v_scr, q_scr, m_scr, l_scr, acc_scr = refs[n_q + n_k + 2:]
    t = o_ref.shape[0]
    _, t_pad, dk = k_scr.shape
    dv = v_scr.shape[2]
    n_full, tail = t // ATT_BLK, t % ATT_BLK
    per_head_k = k_scr.shape[0] > 1
    per_head_v = v_scr.shape[0] > 1

    for j in range(k_scr.shape[0]):
        k_scr[j, pl.ds(t, t_pad - t), :] = jnp.zeros((t_pad - t, dk), k_scr.dtype)
        off = 0
        for r, shared in zip(k_refs, k_shared):
            w = r.shape[1] if shared else r.shape[1] // hs
            src = r[...] if shared else r[:, j * w:(j + 1) * w]
            k_scr[j, 0:t, off:off + w] = src.astype(k_scr.dtype)
            off += w
    for j in range(v_scr.shape[0]):
        v_scr[j, pl.ds(t, t_pad - t), :] = jnp.zeros((t_pad - t, dv), v_scr.dtype)
        src = v_ref[...] if v_shared else v_ref[:, j * dv:(j + 1) * dv]
        v_scr[j, 0:t, :] = src.astype(v_scr.dtype)

    def q_block(rows, tq, n_off, n_masked):
        mrows = hs * tq
        for j in range(hs):
            off = 0
            for r in q_refs:
                w = r.shape[1] // hs
                q_scr[j * tq:(j + 1) * tq, off:off + w] = r[rows, j * w:(j + 1) * w]
                off += w
        m_scr[0:mrows, :] = jnp.full((mrows, LANES), NEG_INF, jnp.float32)
        l_scr[0:mrows, :] = jnp.zeros((mrows, LANES), jnp.float32)
        acc_scr[0:mrows, :] = jnp.zeros((mrows, dv), jnp.float32)

        def attend(kj, mask_shift):
            krows = pl.ds(pl.multiple_of(kj * ATT_BLK, ATT_BLK), ATT_BLK)
            for j in range(hs):
                sl = slice(j * tq, (j + 1) * tq)
                s = lax.dot_general(q_scr[sl, :], k_scr[j if per_head_k else 0, krows, :], NT_DIMS,
                                    preferred_element_type=jnp.float32)
                if mask_shift is not None:
                    r_in = lax.broadcasted_iota(jnp.int32, (tq, ATT_BLK), 0)
                    c_in = lax.broadcasted_iota(jnp.int32, (tq, ATT_BLK), 1) + mask_shift
                    s = jnp.where(c_in <= r_in, s, NEG_INF)
                s0, s1 = s[:, 0:LANES], s[:, LANES:]
                m_old = m_scr[sl, :]
                m_new = jnp.maximum(m_old, jnp.max(jnp.maximum(s0, s1), axis=-1, keepdims=True))
                alpha = jnp.exp(m_old - m_new)
                p0 = jnp.exp(s0 - m_new)
                p1 = jnp.exp(s1 - m_new)
                l_scr[sl, :] = alpha * l_scr[sl, :] + jnp.sum(p0 + p1, axis=-1, keepdims=True)
                m_scr[sl, :] = m_new
                p = jnp.concatenate([p0, p1], axis=1).astype(v_scr.dtype)
                acc_scr[sl, :] = alpha * acc_scr[sl, :] + jnp.dot(
                    p, v_scr[j if per_head_v else 0, krows, :], preferred_element_type=jnp.float32)

        def off_diag(kj, carry):
            attend(kj, None)
            return carry

        lax.fori_loop(0, n_off, off_diag, 0)
        for d in range(n_masked):
            attend(n_off + d, d * ATT_BLK)
        for j in range(hs):
            sl = slice(j * tq, (j + 1) * tq)
            o_ref[rows, j * dv:(j + 1) * dv] = (acc_scr[sl, :] / l_scr[sl, :]).astype(o_ref.dtype)

    def main(qi, carry):
        q_block(pl.ds(pl.multiple_of(qi * ATT_BLK, ATT_BLK), ATT_BLK), ATT_BLK, qi, 1)
        return carry

    lax.fori_loop(0, n_full - 1, main, 0)
    q_block(slice((n_full - 1) * ATT_BLK, t), ATT_BLK + tail, n_full - 1, 2)


def _flash(q_parts, k_parts, v_part, batch, t, n_heads, hs, name, single_buffer=False):
    assert t % ATT_BLK and (t % ATT_BLK) % 16 == 0 and n_heads % hs == 0
    t_pad = _round_up(t, ATT_BLK)
    dv = V_DIM
    big = {"pipeline_mode": pl.Buffered(1)} if single_buffer else {}
    in_specs, operands = [], []
    for arr, w, col0 in q_parts:
        assert col0 % (hs * w) == 0
        in_specs.append(pl.BlockSpec((t, hs * w), lambda b, g, c0=col0 // (hs * w): (b, c0 + g), **big))
        operands.append(arr)

    def kv_spec(part):
        _, w, col0, share = part
        if share:
            assert col0 % w == 0 and share % hs == 0
            return pl.BlockSpec((t, w), lambda b, g: (b, col0 // w + (g * hs) // share)), True
        assert col0 % (hs * w) == 0
        return pl.BlockSpec((t, hs * w), lambda b, g: (b, col0 // (hs * w) + g), **big), False

    shared = []
    for part in list(k_parts) + [v_part]:
        spec, is_shared = kv_spec(part)
        in_specs.append(spec)
        operands.append(part[0])
        shared.append(is_shared)
    v_shared = shared.pop()
    dk = sum(p[1] for p in k_parts)
    mrows = hs * (ATT_BLK + t % ATT_BLK)
    scratch = [pltpu.VMEM((1 if all(shared) else hs, t_pad, dk), jnp.bfloat16),
               pltpu.VMEM((1 if v_shared else hs, t_pad, dv), jnp.bfloat16),
               pltpu.VMEM((mrows, dk), jnp.bfloat16), pltpu.VMEM((mrows, LANES), jnp.float32),
               pltpu.VMEM((mrows, LANES), jnp.float32), pltpu.VMEM((mrows, dv), jnp.float32)]
    return pl.pallas_call(
        functools.partial(_flash_kernel, n_q=len(q_parts), k_shared=tuple(shared), v_shared=v_shared, hs=hs),
        grid=(batch, n_heads // hs),
        in_specs=in_specs,
        out_specs=pl.BlockSpec((t, hs * dv), lambda b, g: (b, g)),
        out_shape=jax.ShapeDtypeStruct((batch * t, n_heads * dv), jnp.float32),
        scratch_shapes=scratch,
        compiler_params=_cparams("parallel", "parallel"),
        name=name,
    )(*operands)


def _head_mm_kernel(a_ref, b_ref, o_ref, *, scale):
    acc = jnp.dot(a_ref[...], b_ref[0], preferred_element_type=jnp.float32)
    if scale != 1.0:
        acc = acc * scale
    o_ref[...] = acc.astype(o_ref.dtype)


def _head_mm(a, w, out_dtype, tm, scale, name):
    m = a.shape[0]
    n_heads, kdim, n = w.shape
    return pl.pallas_call(
        functools.partial(_head_mm_kernel, scale=scale),
        grid=(n_heads, m // tm),
        in_specs=[pl.BlockSpec((tm, kdim), lambda h, i: (i, h)),
                  pl.BlockSpec((1, kdim, n), lambda h, i: (h, 0, 0))],
        out_specs=pl.BlockSpec((tm, n), lambda h, i: (i, h)),
        out_shape=jax.ShapeDtypeStruct((m, n_heads * n), out_dtype),
        compiler_params=_cparams("parallel", "parallel"),
        name=name,
    )(a, w)


PAGES_PER_STEP = 32
Q_ROWS = 64
N_CACHES = 5


def _page_copies(pt_ref, caches, bufs, sems, seq, step, slot, n_steps):
    kc, vc, cc, rc, lc = caches
    kbuf, vbuf, cbuf, rbuf, lbuf = bufs
    npg = PAGES_PER_STEP
    copies = []
    for p in range(npg):
        page = pt_ref[seq, (n_steps - 1 - step) * npg + p]
        rows2 = pl.ds(p * PAGE_SIZE * FOX_KV_HEADS, PAGE_SIZE * FOX_KV_HEADS)
        copies.append(pltpu.make_async_copy(kc.at[page], kbuf.at[slot, rows2, :], sems.at[slot, 0]))
        copies.append(pltpu.make_async_copy(vc.at[page], vbuf.at[slot, rows2, :], sems.at[slot, 1]))
        copies.append(pltpu.make_async_copy(cc.at[page], cbuf.at[slot, pl.ds(p * PAGE_SIZE, PAGE_SIZE), :],
                                            sems.at[slot, 2]))
        copies.append(pltpu.make_async_copy(rc.at[page], rbuf.at[slot, :, pl.ds(p * PAGE_SIZE, PAGE_SIZE)],
                                            sems.at[slot, 3]))
        copies.append(pltpu.make_async_copy(lc.at[page], lbuf.at[slot, pl.ds(p * FOX_HEADS, FOX_HEADS), :],
                                            sems.at[slot, 4]))
    return copies


def _decode_kernel(pt_ref, kc, vc, cc, rc, lc, kn_ref, vn_ref, cn_ref, rn_ref, ln_ref, qf_ref, qa_ref, qr_ref,
                   of_ref, ol_ref,
                   kbuf, vbuf, cbuf, rbuf, lbuf, sems, mf_scr, lf_scr, af_scr, mm_scr, lm_scr, am_scr, run_scr,
                   qt_scr, *, n_steps):
    caches = (kc, vc, cc, rc, lc)
    bufs = (kbuf, vbuf, cbuf, rbuf, lbuf)
    seq = pl.program_id(0)
    step = pl.program_id(1)
    n_seq = pl.num_programs(0)
    slot = step % 2
    half = Q_ROWS // FOX_KV_HEADS
    n_tok = half // FOX_GROUP

    @pl.when(jnp.logical_and(seq == 0, step == 0))
    def _():
        for cp in _page_copies(pt_ref, caches, bufs, sems, seq, step, slot, n_steps):
            cp.start()

    last = jnp.logical_and(seq == n_seq - 1, step == n_steps - 1)

    @pl.when(jnp.logical_not(last))
    def _():
        wrap = step == n_steps - 1
        nseq = jnp.where(wrap, seq + 1, seq)
        nstep = jnp.where(wrap, 0, step + 1)
        for cp in _page_copies(pt_ref, caches, bufs, sems, nseq, nstep, 1 - slot, n_steps):
            cp.start()

    r = lax.broadcasted_iota(jnp.int32, (LANES, 2 * LANES), 0)
    c = lax.broadcasted_iota(jnp.int32, (LANES, 2 * LANES), 1)
    sfx_tot = ((r > c) | (c >= LANES)).astype(jnp.bfloat16)

    def attend(n_pages, lf_all, k_of, v_of, ckv_all, kr_all, is_new):
        width = n_pages * LANES
        st = _dot_f32_by01(lf_all, sfx_tot)
        run = run_scr[...]
        g_pages = [None] * n_pages
        for p in reversed(range(n_pages)):
            rows = slice(p * FOX_HEADS, (p + 1) * FOX_HEADS)
            g_pages[p] = run + st[rows, 0:LANES]
            run = run + st[rows, LANES:]
        run_scr[...] = run
        if is_new:
            for kvh in range(FOX_KV_HEADS):
                for i in range(n_tok):
                    col = g_pages[0][kvh * FOX_GROUP:(kvh + 1) * FOX_GROUP, i:i + 1]
                    row0 = kvh * half + i * FOX_GROUP
                    qt_scr[row0:row0 + FOX_GROUP, :] = jnp.broadcast_to(-col, (FOX_GROUP, LANES))
        slot_idx = lax.broadcasted_iota(jnp.int32, (1, width), 1)

        for kvh in range(FOX_KV_HEADS):
            rows = slice(kvh * half, (kvh + 1) * half)
            k = k_of(kvh).astype(jnp.bfloat16)
            v = v_of(kvh).astype(jnp.bfloat16)
            s = lax.dot_general(qf_ref[0, rows, :], k, NT_DIMS, preferred_element_type=jnp.float32)
            hrows = slice(kvh * FOX_GROUP, (kvh + 1) * FOX_GROUP)
            g = g_pages[0][hrows, :] if n_pages == 1 else jnp.concatenate([gp[hrows, :] for gp in g_pages], axis=1)
            s = s + jnp.concatenate([g] * n_tok, axis=0) + qt_scr[rows, 0:1]
            if is_new:
                tok = lax.broadcasted_iota(jnp.int32, (half, width), 0) // FOX_GROUP
                s = jnp.where(slot_idx <= tok, s, NEG_INF)
            m_old = mf_scr[rows, :]
            m_new = jnp.maximum(m_old, jnp.max(s, axis=-1, keepdims=True))
            alpha = jnp.exp(m_old - m_new)
            pr = jnp.exp(s - m_new[:, 0:1])
            lf_scr[rows, :] = alpha * lf_scr[rows, :] + jnp.sum(pr, axis=-1, keepdims=True)
            mf_scr[rows, :] = m_new
            af_scr[rows, :] = alpha * af_scr[rows, :] + jnp.dot(pr.astype(jnp.bfloat16), v,
                                                               preferred_element_type=jnp.float32)

        ckv = ckv_all.astype(jnp.bfloat16)
        s = (lax.dot_general(qa_ref[0], ckv, NT_DIMS, preferred_element_type=jnp.float32)
             + jnp.dot(qr_ref[0], kr_all.astype(jnp.bfloat16), preferred_element_type=jnp.float32))
        if is_new:
            tok = lax.broadcasted_iota(jnp.int32, (Q_ROWS, width), 0) % n_tok
            s = jnp.where(slot_idx <= tok, s, NEG_INF)
        m_old = mm_scr[...]
        m_new = jnp.maximum(m_old, jnp.max(s, axis=-1, keepdims=True))
        alpha = jnp.exp(m_old - m_new)
        pr = jnp.exp(s - m_new[:, 0:1])
        lm_scr[...] = alpha * lm_scr[...] + jnp.sum(pr, axis=-1, keepdims=True)
        mm_scr[...] = m_new
        am_scr[...] = alpha[:, 0:1] * am_scr[...] + jnp.dot(pr.astype(jnp.bfloat16), ckv,
                                                           preferred_element_type=jnp.float32)

    @pl.when(step == 0)
    def _():
        mf_scr[...] = jnp.full_like(mf_scr, NEG_INF)
        lf_scr[...] = jnp.zeros_like(lf_scr)
        af_scr[...] = jnp.zeros_like(af_scr)
        mm_scr[...] = jnp.full_like(mm_scr, NEG_INF)
        lm_scr[...] = jnp.zeros_like(lm_scr)
        am_scr[...] = jnp.zeros_like(am_scr)
        run_scr[...] = jnp.zeros_like(run_scr)
        strided = lambda ref: (lambda kvh: ref[0, pl.ds(kvh, PAGE_SIZE, stride=FOX_KV_HEADS), :])
        attend(1, ln_ref[0], strided(kn_ref), strided(vn_ref), cn_ref[0], rn_ref[0], True)

    for cp in _page_copies(pt_ref, caches, bufs, sems, seq, step, slot, n_steps):
        cp.wait()

    npg = PAGES_PER_STEP
    strided_buf = lambda buf: (lambda kvh: buf[slot, pl.ds(kvh, npg * PAGE_SIZE, stride=FOX_KV_HEADS), :])
    attend(npg, lbuf[slot], strided_buf(kbuf), strided_buf(vbuf), cbuf[slot], rbuf[slot], False)

    @pl.when(step == n_steps - 1)
    def _():
        of_ref[0] = af_scr[...] / lf_scr[...]
        ol_ref[0] = (am_scr[...] / lm_scr[:, 0:1]).astype(ol_ref.dtype)


def _decode(page_table, caches, news, qf, qa, qr):
    n_seq, n_pages = page_table.shape
    npg = PAGES_PER_STEP
    n_steps = n_pages // npg
    assert n_pages % npg == 0 and n_steps % 2 == 0
    per_seq = lambda arr: pl.BlockSpec((1,) + tuple(arr.shape[1:]), lambda b, s, pt: (b, 0, 0))
    any_spec = pl.BlockSpec(memory_space=pl.ANY)
    in_specs = [any_spec] * N_CACHES + [per_seq(a) for a in news] + [per_seq(a) for a in (qf, qa, qr)]
    rows2 = npg * PAGE_SIZE * FOX_KV_HEADS
    grid_spec = pltpu.PrefetchScalarGridSpec(
        num_scalar_prefetch=1,
        grid=(n_seq, n_steps),
        in_specs=in_specs,
        out_specs=[pl.BlockSpec((1, Q_ROWS, HEAD_DIM), lambda b, s, pt: (b, 0, 0)),
                   pl.BlockSpec((1, Q_ROWS, KV_LORA), lambda b, s, pt: (b, 0, 0))],
        scratch_shapes=[pltpu.VMEM((2, rows2, HEAD_DIM), jnp.float32), pltpu.VMEM((2, rows2, HEAD_DIM), jnp.float32),
                        pltpu.VMEM((2, npg * PAGE_SIZE, KV_LORA), jnp.float32),
                        pltpu.VMEM((2, ROPE_DIM, npg * PAGE_SIZE), jnp.float32),
                        pltpu.VMEM((2, npg * FOX_HEADS, LANES), jnp.float32),
                        pltpu.SemaphoreType.DMA((2, N_CACHES)),
                        pltpu.VMEM((Q_ROWS, LANES), jnp.float32), pltpu.VMEM((Q_ROWS, LANES), jnp.float32),
                        pltpu.VMEM((Q_ROWS, HEAD_DIM), jnp.float32),
                        pltpu.VMEM((Q_ROWS, LANES), jnp.float32), pltpu.VMEM((Q_ROWS, LANES), jnp.float32),
                        pltpu.VMEM((Q_ROWS, KV_LORA), jnp.float32),
                        pltpu.VMEM((FOX_HEADS, LANES), jnp.float32), pltpu.VMEM((Q_ROWS, LANES), jnp.float32)],
    )
    return pl.pallas_call(
        functools.partial(_decode_kernel, n_steps=n_steps),
        grid_spec=grid_spec,
        out_shape=[jax.ShapeDtypeStruct((n_seq, Q_ROWS, HEAD_DIM), jnp.float32),
                   jax.ShapeDtypeStruct((n_seq, Q_ROWS, KV_LORA), jnp.bfloat16)],
        compiler_params=_cparams("arbitrary", "arbitrary"),
        name="decode_attention",
    )(page_table, *caches, *news, qf, qa, qr)


def _rope_tables(pos):
    half = ROPE_DIM // 2
    inv = ROPE_BASE ** (-jnp.arange(half, dtype=jnp.float32) / half)
    ang = pos.astype(jnp.float32)[:, None] * inv[None, :]
    cos, sin = jnp.cos(ang), jnp.sin(ang)
    z = jnp.zeros_like(cos)
    pad = jnp.zeros((pos.shape[0], LANES - ROPE_DIM), jnp.float32)
    cos_t = jnp.concatenate([cos, cos, pad], axis=1)
    sa_t = jnp.concatenate([-sin, z, pad], axis=1)
    sb_t = jnp.concatenate([z, sin, pad], axis=1)
    return cos_t, sa_t, sb_t


def _prep_weights(w_in, b_f, w_q_up, w_uk, w_uv, w_o, w_up, w_down):
    bf = jnp.bfloat16
    offs = [0]
    for s in (FOX_W, FOX_KV_HEADS * HEAD_DIM, FOX_KV_HEADS * HEAD_DIM, FOX_HEADS, Q_LORA, KV_LORA, ROPE_DIM):
        offs.append(offs[-1] + s)
    q_f, k_f, v_f, f_l, q_lat, c_kv, k_r = [w_in[:, offs[i]:offs[i + 1]] for i in range(7)]
    zc = lambda n: jnp.zeros((D_MODEL, n), w_in.dtype)
    w_in_p = jnp.concatenate([q_f, q_lat, c_kv, k_f, v_f, k_r, zc(LANES - ROPE_DIM), f_l, zc(LANES - FOX_HEADS),
                              zc(Z_W - Z_USED)], axis=1).astype(bf)
    b_f_pad = jnp.concatenate([b_f, jnp.zeros((LANES - FOX_HEADS,), b_f.dtype)]).reshape(1, LANES)
    wq = w_q_up.reshape(Q_LORA, MLA_HEADS, NOPE_DIM + ROPE_DIM)
    wq_nope = wq[:, :, :NOPE_DIM].reshape(Q_LORA, MLA_HEADS * NOPE_DIM)
    wq_rope = jnp.concatenate([wq[:, :, NOPE_DIM:], jnp.zeros((Q_LORA, MLA_HEADS, LANES - ROPE_DIM), wq.dtype)],
                              axis=2).reshape(Q_LORA, MLA_HEADS * LANES)
    wq_p = jnp.concatenate([wq_nope, wq_rope], axis=1).astype(bf)
    wk_all = w_uk.transpose(1, 0, 2).reshape(KV_LORA, MLA_HEADS * NOPE_DIM)
    wv_all = w_uv.transpose(1, 0, 2).reshape(KV_LORA, MLA_W)
    w_kv = jnp.concatenate([wk_all, wv_all], axis=1).astype(bf)
    w_uk_t = w_uk.transpose(0, 2, 1).astype(bf)
    return w_in_p, b_f_pad, wq_p, w_kv, w_uk_t, w_uv.astype(bf), w_o, w_up, w_down


def _dense_tail(x, o_f, o_m, g, wts, tm, tr, tag):
    w_o, w_up, w_down = wts
    o = _fin1(o_f, o_m, g["fox_out"], g["mla_out"], tr, "fin1_" + tag)
    y = _matmul(o, w_o, jnp.float32, tm, 512, name="mm_o_" + tag)
    x1, h2 = _fin2(x, y, g["post_mix"], g["pre_mlp"], tr, "fin2_" + tag)
    u = _matmul(h2, w_up, jnp.bfloat16, tm, 512, relu2=True, name="mm_up_" + tag)
    d = _matmul(u, w_down, jnp.float32, tm, 1024, tk=2048, name="mm_down_" + tag)
    return _fin3(x1, d, g["post_mlp"], tr, "fin3_" + tag)


def kernel(x_prompt, x_sample, cache_fox_k, cache_fox_v, cache_fox_logf, cache_mla_ckv, cache_mla_krope,
           page_table, meta_tokens, g_pre_mix, w_in, b_f, g_q_lat, w_q_up, w_uk, g_kv_lat, w_uv, g_fox_out,
           g_mla_out, w_o, g_post_mix, g_pre_mlp, w_up, w_down, g_post_mlp):
    depth = w_in.shape[0]
    assert depth == 1
    batch, seq, _ = x_prompt.shape
    t = seq + N_META
    n_seq, dec_seq, _ = x_sample.shape
    n_past = page_table.shape[1] * PAGE_SIZE
    assert dec_seq * FOX_HEADS == Q_ROWS

    (w_in_p, b_f_pad, wq_p, w_kv, w_uk_t, w_uv_b, w_o_b, w_up_b, w_down_b) = _prep_weights(
        w_in[0], b_f[0], w_q_up[0], w_uk[0], w_uv[0], w_o[0], w_up[0], w_down[0])
    g = {"fox_out": g_fox_out[0], "mla_out": g_mla_out[0], "post_mix": g_post_mix[0],
         "pre_mlp": g_pre_mlp[0], "post_mlp": g_post_mlp[0]}
    tail_w = (w_o_b, w_up_b, w_down_b)

    meta = jnp.broadcast_to(meta_tokens[None].astype(x_prompt.dtype), (batch, N_META, D_MODEL))
    xp = jnp.concatenate([meta, x_prompt], axis=1).reshape(batch * t, D_MODEL)
    tm_p, tr_wide, tr_p = 1376, 192, 688
    tabs_p = _rope_tables(jnp.arange(t))
    h = _rms_cast(xp, g_pre_mix[0], tr_wide, "rms_in_p")
    z = _matmul(h, w_in_p, jnp.float32, tm_p, 512, name="mm_in_p")
    qlat, ckv, ckvb, kr, krb, logf = _post_in(z, tabs_p, t // tr_p, g_q_lat[0], g_kv_lat[0], b_f_pad, tr_p,
                                              "post_in_p")
    q_all = _q_up(qlat, wq_p, tabs_p, t // tr_p, tr_p, 1024, MLA_SCALE, "q_up_p")
    kv_up = _matmul(ckvb, w_kv, jnp.bfloat16, tm_p, 1024, name="mm_kvup_p")
    o_f, o_m = _prompt_attention(z, logf, q_all, kv_up, krb, batch, t, tr_p)
    yp = _dense_tail(xp, o_f, o_m, g, tail_w, tm_p, tr_wide, "p")

    k_f_p = z[:, Z_KF:Z_KF + FOX_KV_HEADS * HEAD_DIM].reshape(1, batch, t, FOX_KV_HEADS, HEAD_DIM)
    v_f_p = z[:, Z_VF:Z_VF + FOX_KV_HEADS * HEAD_DIM].reshape(1, batch, t, FOX_KV_HEADS, HEAD_DIM)
    logf_p = logf[:, :FOX_HEADS].reshape(1, batch, t, FOX_HEADS)
    ckv_p = ckv.reshape(1, batch, t, KV_LORA)
    kr_p = kr[:, :ROPE_DIM].reshape(1, batch, t, ROPE_DIM)
    y_prompt = yp.reshape(batch, t, D_MODEL)[:, N_META:]

    ms = n_seq * dec_seq
    xs = x_sample.reshape(ms, D_MODEL)
    tabs_s = _rope_tables(n_past + jnp.arange(dec_seq))
    tabs_s = tuple(jnp.tile(tb, (ms // dec_seq, 1)) for tb in tabs_s)
    hs = _rms_cast(xs, g_pre_mix[0], 128, "rms_in_s")
    zs = _matmul(hs, w_in_p, jnp.float32, ms, 512, name="mm_in_s")
    qlat_s, ckv_s, _, kr_s, _, logf_s = _post_in(zs, tabs_s, 1, g_q_lat[0], g_kv_lat[0], b_f_pad, ms, "post_in_s")
    q_all_s = _q_up(qlat_s, wq_p, tabs_s, 1, ms, 1024, 1.0, "q_up_s")
    qr_s = q_all_s[:, MLA_HEADS * NOPE_DIM:]
    qa_s = _head_mm(q_all_s, w_uk_t, jnp.bfloat16, ms, MLA_SCALE, "q_absorb_s")

    kf_s = zs[:, Z_KF:Z_KF + FOX_KV_HEADS * HEAD_DIM]
    vf_s = zs[:, Z_VF:Z_VF + FOX_KV_HEADS * HEAD_DIM]
    o_f_s, ol_s = _sample_attention(zs, qa_s, qr_s, ckv_s, kr_s, logf_s, cache_fox_k[0], cache_fox_v[0],
                                    cache_fox_logf[0], cache_mla_ckv[0], cache_mla_krope[0], page_table,
                                    n_seq, dec_seq)
    o_m_s = _head_mm(ol_s, w_uv_b, jnp.float32, ms, 1.0, "o_unabsorb_s")
    ys = _dense_tail(xs, o_f_s, o_m_s, g, tail_w, ms, 128, "s")

    k_f_s = kf_s.reshape(1, n_seq, dec_seq, FOX_KV_HEADS, HEAD_DIM)
    v_f_s = vf_s.reshape(1, n_seq, dec_seq, FOX_KV_HEADS, HEAD_DIM)
    logf_s_o = logf_s[:, :FOX_HEADS].reshape(1, n_seq, dec_seq, FOX_HEADS)
    ckv_s_o = ckv_s.reshape(1, n_seq, dec_seq, KV_LORA)
    kr_s_o = kr_s[:, :ROPE_DIM].reshape(1, n_seq, dec_seq, ROPE_DIM)
    y_sample = ys.reshape(n_seq, dec_seq, D_MODEL)

    return (y_prompt, y_sample, k_f_p, v_f_p, logf_p, ckv_p, kr_p, k_f_s, v_f_s, logf_s_o, ckv_s_o, kr_s_o)


def _prompt_attention(z, logf, q_all, kv_up, krb, batch, t, tr):
    cum = _cum_prompt(logf, batch, t)
    q_aug, k_aug = _fox_aug(z, cum, tr)
    o_f = _flash([(q_aug, AUG_W, 0)], [(k_aug, AUG_W, 0, FOX_GROUP)], (z, HEAD_DIM, Z_VF, FOX_GROUP), batch, t,
                 FOX_HEADS, 8, "flash_prompt_fox")
    o_m = _flash([(q_all, NOPE_DIM, 0), (q_all, LANES, MLA_HEADS * NOPE_DIM)],
                 [(kv_up, NOPE_DIM, 0, 0), (krb, LANES, 0, MLA_HEADS)],
                 (kv_up, V_DIM, MLA_HEADS * NOPE_DIM, 0), batch, t, MLA_HEADS, 8, "flash_prompt_mla",
                 single_buffer=True)
    return o_f, o_m


def _sample_attention(zs, qa_s, qr_s, ckv_s, kr_s, logf_s, cache_k, cache_v, cache_lf, cache_ckv, cache_kr,
                      page_table, n_seq, dec_seq):
    ms = n_seq * dec_seq
    kf_s = zs[:, Z_KF:Z_KF + FOX_KV_HEADS * HEAD_DIM]
    vf_s = zs[:, Z_VF:Z_VF + FOX_KV_HEADS * HEAD_DIM]
    qf_d = (zs[:, Z_QF:Z_QF + FOX_W] * FOX_SCALE).astype(jnp.bfloat16)
    qf_d = qf_d.reshape(n_seq, dec_seq, FOX_KV_HEADS, FOX_GROUP, HEAD_DIM)
    qf_d = qf_d.transpose(0, 2, 1, 3, 4).reshape(n_seq, Q_ROWS, HEAD_DIM)
    qa_d = qa_s.reshape(n_seq, dec_seq, MLA_HEADS, KV_LORA).transpose(0, 2, 1, 3).reshape(n_seq, Q_ROWS, KV_LORA)
    qr_d = qr_s.reshape(n_seq, dec_seq, MLA_HEADS, LANES)[..., :ROPE_DIM].transpose(0, 2, 1, 3)
    qr_d = qr_d.reshape(n_seq, Q_ROWS, ROPE_DIM)
    pad_slots = lambda a, rows: jnp.pad(a, ((0, 0), (0, rows - a.shape[1]), (0, 0)))
    pad_lanes = lambda a: jnp.pad(a, ((0, 0), (0, 0), (0, PAGE_SIZE - a.shape[2])))
    k_new = pad_slots(kf_s.reshape(n_seq, dec_seq * FOX_KV_HEADS, HEAD_DIM), PAGE_SIZE * FOX_KV_HEADS)
    v_new = pad_slots(vf_s.reshape(n_seq, dec_seq * FOX_KV_HEADS, HEAD_DIM), PAGE_SIZE * FOX_KV_HEADS)
    ckv_new = pad_slots(ckv_s.reshape(n_seq, dec_seq, KV_LORA), PAGE_SIZE)
    kr_new = pad_lanes(kr_s[:, :ROPE_DIM].reshape(n_seq, dec_seq, ROPE_DIM).transpose(0, 2, 1))
    lf_new = pad_lanes(logf_s[:, :FOX_HEADS].reshape(n_seq, dec_seq, FOX_HEADS).transpose(0, 2, 1))
    n_pool = cache_k.shape[0]
    caches = (cache_k.reshape(n_pool, PAGE_SIZE * FOX_KV_HEADS, HEAD_DIM),
              cache_v.reshape(n_pool, PAGE_SIZE * FOX_KV_HEADS, HEAD_DIM),
              cache_ckv, cache_kr.transpose(0, 2, 1), cache_lf.transpose(0, 2, 1))
    of_d, ol_d = _decode(page_table, caches, (k_new, v_new, ckv_new, kr_new, lf_new), qf_d, qa_d, qr_d)
    o_f_s = of_d.reshape(n_seq, FOX_KV_HEADS, dec_seq, FOX_GROUP, HEAD_DIM).transpose(0, 2, 1, 3, 4)
    o_f_s = o_f_s.reshape(ms, FOX_W)
    ol_s = ol_d.reshape(n_seq, MLA_HEADS, dec_seq, KV_LORA).transpose(0, 2, 1, 3).reshape(ms, MLA_HEADS * KV_LORA)
    return o_f_s, ol_s
```
